```python
import math
import jax, jax.numpy as jnp
from jax import lax
import numpy as np

D_MODEL = 1024
BATCH = 8
SEQ = 4096
DEPTH = 4

N_MIXERS = 3
D_FF = 2816
EPS = 1e-6
D_RNN = D_MODEL
RG_BLOCK = 128
RG_NBLOCKS = D_RNN // RG_BLOCK
RG_CONV = 4
RG_C = 8.0
FOX_HEADS = 16
FOX_HEAD_DIM = D_MODEL // FOX_HEADS
Q_BLOCK = 128
CV_KERNEL = 31

kernel_name = 'hybrid_rglru_fox_conformer_macaron'


def _rms_norm(x, g):
    xf = x.astype(jnp.float32)
    y = xf * lax.rsqrt(jnp.mean(xf * xf, axis=-1, keepdims=True) + EPS)
    return (y * g.astype(jnp.float32)).astype(x.dtype)


def _layer_norm(x, g, b):
    xf = x.astype(jnp.float32)
    mu = jnp.mean(xf, axis=-1, keepdims=True)
    var = jnp.mean(jnp.square(xf - mu), axis=-1, keepdims=True)
    y = (xf - mu) * lax.rsqrt(var + EPS)
    return (y * g.astype(jnp.float32) + b.astype(jnp.float32)).astype(x.dtype)


def _swiglu(x, w_in, w_out):
    gate, up = jnp.split(x @ w_in, 2, axis=-1)
    return (jax.nn.silu(gate) * up) @ w_out


def _causal_dw_conv(x, w, b):
    k_width, channels = w.shape
    y = lax.conv_general_dilated(
        x, w[:, None, :].astype(x.dtype), window_strides=(1,),
        padding=[(k_width - 1, 0)], dimension_numbers=('NWC', 'WIO', 'NWC'),
        feature_group_count=channels)
    return y + b


def _rglru_mixer(x, w_in, conv_w, conv_b, w_a, b_a, w_x, b_x, lam, w_out):
    bsz, seq, _ = x.shape
    gate, u = jnp.split(x @ w_in, 2, axis=-1)
    u = _causal_dw_conv(u, conv_w, conv_b)
    ub = u.reshape(bsz, seq, RG_NBLOCKS, RG_BLOCK)
    r = jax.nn.sigmoid(jnp.einsum('bsnc,ncd->bsnd', ub, w_a).reshape(bsz, seq, D_RNN) + b_a)
    i = jax.nn.sigmoid(jnp.einsum('bsnc,ncd->bsnd', ub, w_x).reshape(bsz, seq, D_RNN) + b_x)
    log_a = -RG_C * r.astype(jnp.float32) * jax.nn.softplus(-lam.astype(jnp.float32))
    a = jnp.exp(log_a)
    b = jnp.sqrt(-jnp.expm1(2.0 * log_a)) * (i * u).astype(jnp.float32)

    def combine(left, right):
        a_l, b_l = left
        a_r, b_r = right
        return a_l * a_r, a_r * b_l + b_r

    _, h = lax.associative_scan(combine, (a, b), axis=1)
    y = h.astype(x.dtype) * jax.nn.gelu(gate)
    return y @ w_out


def _fox_mixer(x, w_in, b_f, q_g, k_g, w_out):
    bsz, seq, _ = x.shape
    q, k, v, f_logit = jnp.split(x @ w_in, [D_MODEL, 2 * D_MODEL, 3 * D_MODEL], axis=-1)
    q = _rms_norm(q.reshape(bsz, seq, FOX_HEADS, FOX_HEAD_DIM), q_g).transpose(0, 2, 1, 3)
    k = _rms_norm(k.reshape(bsz, seq, FOX_HEADS, FOX_HEAD_DIM), k_g).transpose(0, 2, 1, 3)
    v = v.reshape(bsz, seq, FOX_HEADS, FOX_HEAD_DIM).transpose(0, 2, 1, 3)
    log_f = jax.nn.log_sigmoid(f_logit.astype(jnp.float32) + b_f.astype(jnp.float32))
    cum = jnp.cumsum(log_f, axis=1).transpose(0, 2, 1)
    scale = FOX_HEAD_DIM ** -0.5
    outs = []
    for blk in range(seq // Q_BLOCK):
        q0 = blk * Q_BLOCK
        q1 = q0 + Q_BLOCK
        s = jnp.einsum('bhqd,bhkd->bhqk', q[:, :, q0:q1], k[:, :, :q1]).astype(jnp.float32) * scale
        s = s + cum[:, :, q0:q1, None] - cum[:, :, None, :q1]
        mask = (q0 + jnp.arange(Q_BLOCK))[:, None] >= jnp.arange(q1)[None, :]
        s = jnp.where(mask, s, -jnp.inf)
        p = jax.nn.softmax(s, axis=-1).astype(v.dtype)
        outs.append(jnp.einsum('bhqk,bhkd->bhqd', p, v[:, :, :q1]))
    o = jnp.concatenate(outs, axis=2).transpose(0, 2, 1, 3).reshape(bsz, seq, D_MODEL)
    return o @ w_out


def _conv_module(x, w_in, b_in, dw_w, dw_b, ln_g, ln_b, w_out, b_out):
    val, gate = jnp.split(x @ w_in + b_in, 2, axis=-1)
    h = val * jax.nn.sigmoid(gate)
    h = _causal_dw_conv(h, dw_w, dw_b)
    h = jax.nn.silu(_layer_norm(h, ln_g, ln_b))
    return h @ w_out + b_out


def setup_inputs(seed: int = 0) -> dict:
    key = jax.random.key(seed)
    ks = iter(jax.random.split(key, 48))

    def normal(shape, scale):
        return scale * jax.random.normal(next(ks), shape, jnp.float32)

    n_a = (DEPTH + 2) // 3
    n_b = (DEPTH + 1) // 3
    n_c = DEPTH // 3
    out_scale = (2.0 * DEPTH) ** -0.5
    d = D_MODEL

    x = normal((BATCH, SEQ, d), 1.0)
    ffn_norm = 1.0 + normal((DEPTH, 2, d), 0.02)
    ffn_w_in = normal((DEPTH, 2, d, 2 * D_FF), d ** -0.5)
    ffn_w_out = normal((DEPTH, 2, D_FF, d), D_FF ** -0.5 * out_scale)
    mix_norm = 1.0 + normal((DEPTH, d), 0.02)

    rg_w_in = normal((n_a, d, 2 * D_RNN), d ** -0.5)
    rg_conv_w = normal((n_a, RG_CONV, D_RNN), RG_CONV ** -0.5)
    rg_conv_b = normal((n_a, D_RNN), 0.02)
    rg_w_a = normal((n_a, RG_NBLOCKS, RG_BLOCK, RG_BLOCK), RG_BLOCK ** -0.5)
    rg_b_a = normal((n_a, D_RNN), 0.02)
    rg_w_x = normal((n_a, RG_NBLOCKS, RG_BLOCK, RG_BLOCK), RG_BLOCK ** -0.5)
    rg_b_x = normal((n_a, D_RNN), 0.02)
    a_pow_c = jax.random.uniform(next(ks), (n_a, D_RNN), jnp.float32, 0.9, 0.999)
    sig_l = a_pow_c ** (1.0 / RG_C)
    rg_lambda = jnp.log(sig_l) - jnp.log1p(-sig_l)
    rg_w_out = normal((n_a, D_RNN, d), D_RNN ** -0.5 * out_scale)

    fox_w_in = normal((n_b, d, 3 * d + FOX_HEADS), d ** -0.5)
    fox_b_f = jax.random.uniform(next(ks), (n_b, FOX_HEADS), jnp.float32, 1.0, 6.0)
    fox_q_norm = 1.0 + normal((n_b, FOX_HEAD_DIM), 0.02)
    fox_k_norm = 1.0 + normal((n_b, FOX_HEAD_DIM), 0.02)
    fox_w_out = normal((n_b, d, d), d ** -0.5 * out_scale)

    cv_w_in = normal((n_c, d, 2 * d), d ** -0.5)
    cv_b_in = normal((n_c, 2 * d), 0.02)
    cv_dw_w = normal((n_c, CV_KERNEL, d), CV_KERNEL ** -0.5)
    cv_dw_b = normal((n_c, d), 0.02)
    cv_ln_g = 1.0 + normal((n_c, d), 0.02)
    cv_ln_b = normal((n_c, d), 0.02)
    cv_w_out = normal((n_c, d, d), d ** -0.5 * out_scale)
    cv_b_out = normal((n_c, d), 0.02)

    return {
        'x': x, 'ffn_norm': ffn_norm, 'ffn_w_in': ffn_w_in, 'ffn_w_out': ffn_w_out,
        'mix_norm': mix_norm,
        'rg_w_in': rg_w_in, 'rg_conv_w': rg_conv_w, 'rg_conv_b': rg_conv_b,
        'rg_w_a': rg_w_a, 'rg_b_a': rg_b_a, 'rg_w_x': rg_w_x, 'rg_b_x': rg_b_x,
        'rg_lambda': rg_lambda, 'rg_w_out': rg_w_out,
        'fox_w_in': fox_w_in, 'fox_b_f': fox_b_f, 'fox_q_norm': fox_q_norm,
        'fox_k_norm': fox_k_norm, 'fox_w_out': fox_w_out,
        'cv_w_in': cv_w_in, 'cv_b_in': cv_b_in, 'cv_dw_w': cv_dw_w, 'cv_dw_b': cv_dw_b,
        'cv_ln_g': cv_ln_g, 'cv_ln_b': cv_ln_b, 'cv_w_out': cv_w_out, 'cv_b_out': cv_b_out,
    }


def reference(x, ffn_norm, ffn_w_in, ffn_w_out, mix_norm,
              rg_w_in, rg_conv_w, rg_conv_b, rg_w_a, rg_b_a, rg_w_x, rg_b_x, rg_lambda, rg_w_out,
              fox_w_in, fox_b_f, fox_q_norm, fox_k_norm, fox_w_out,
              cv_w_in, cv_b_in, cv_dw_w, cv_dw_b, cv_ln_g, cv_ln_b, cv_w_out, cv_b_out):
    for i in range(DEPTH):
        kind = i % N_MIXERS
        j = i // N_MIXERS
        x = x + 0.5 * _swiglu(_rms_norm(x, ffn_norm[i, 0]), ffn_w_in[i, 0], ffn_w_out[i, 0])
        h = _rms_norm(x, mix_norm[i])
        if kind == 0:
            h = _rglru_mixer(h, rg_w_in[j], rg_conv_w[j], rg_conv_b[j], rg_w_a[j], rg_b_a[j],
                             rg_w_x[j], rg_b_x[j], rg_lambda[j], rg_w_out[j])
        elif kind == 1:
            h = _fox_mixer(h, fox_w_in[j], fox_b_f[j], fox_q_norm[j], fox_k_norm[j], fox_w_out[j])
        else:
            h = _conv_module(h, cv_w_in[j], cv_b_in[j], cv_dw_w[j], cv_dw_b[j],
                             cv_ln_g[j], cv_ln_b[j], cv_w_out[j], cv_b_out[j])
        x = x + h
        x = x + 0.5 * _swiglu(_rms_norm(x, ffn_norm[i, 1]), ffn_w_in[i, 1], ffn_w_out[i, 1])
    return x
```

```python
import functools

import jax
import jax.numpy as jnp
from jax import lax
from jax.experimental import pallas as pl
from jax.experimental.pallas import tpu as pltpu

F32 = jnp.float32
BF16 = jnp.bfloat16

EPS = 1e-6
N_MIXERS = 3
RG_C = 8.0

LANES = 128
SUBLANES = 8
VMEM_LIMIT_BYTES = 56 * 1024 * 1024

NEG_BIG = -1e30


def _dot(a, b):
    return jnp.dot(a, b, preferred_element_type=F32)


def _dot_nt(a, b):
    return lax.dot_general(a, b, (((1,), (1,)), ((), ())), preferred_element_type=F32)


def _sigmoid(x):
    return 1.0 / (1.0 + jnp.exp(-x))


def _rms(x, g):
    ms = jnp.mean(x * x, axis=-1, keepdims=True)
    return x * lax.rsqrt(ms + EPS) * g


def _softplus(z):
    return jnp.maximum(z, 0.0) + jnp.log1p(jnp.exp(-jnp.abs(z)))


def _params(*semantics):
    return pltpu.CompilerParams(dimension_semantics=semantics, vmem_limit_bytes=VMEM_LIMIT_BYTES)


def _const_spec(shape):
    n = len(shape)
    return pl.BlockSpec(shape, lambda *_: (0,) * n, pipeline_mode=pl.Buffered(1))


def _layer_spec(shape, lead):
    n = len(shape)
    block = (None,) * len(lead) + tuple(shape)
    return pl.BlockSpec(block, lambda *_: tuple(lead) + (0,) * n, pipeline_mode=pl.Buffered(1))


FFN_TM = 512
FFN_TF = 256


def _ffn_kernel(x_ref, g_ref, win_ref, wout_ref, o_ref, *, d_ff, tf):
    x = x_ref[...]
    h = _rms(x, g_ref[...]).astype(BF16)
    acc = None
    for c in range(d_ff // tf):
        gate = _dot(h, win_ref[:, c * tf:(c + 1) * tf])
        up = _dot(h, win_ref[:, d_ff + c * tf:d_ff + (c + 1) * tf])
        a = (gate * _sigmoid(gate) * up).astype(BF16)
        part = _dot(a, wout_ref[c * tf:(c + 1) * tf, :])
        acc = part if acc is None else acc + part
    o_ref[...] = x + 0.5 * acc


def _ffn(x2, g, w_in, w_out, lead):
    t, d = x2.shape
    d_ff = w_out.shape[-2]
    assert t % FFN_TM == 0 and d_ff % FFN_TF == 0
    return pl.pallas_call(
        functools.partial(_ffn_kernel, d_ff=d_ff, tf=FFN_TF),
        grid=(t // FFN_TM,),
        in_specs=[
            pl.BlockSpec((FFN_TM, d), lambda i: (i, 0)),
            _const_spec((1, d)),
            _layer_spec((d, 2 * d_ff), lead),
            _layer_spec((d_ff, d), lead),
        ],
        out_specs=pl.BlockSpec((FFN_TM, d), lambda i: (i, 0)),
        out_shape=jax.ShapeDtypeStruct((t, d), F32),
        compiler_params=_params("parallel"),
        name="ffn",
    )(x2, g, w_in, w_out)


LIN_TM = 512


def _linres_kernel(x_ref, o_ref, w_ref, out_ref):
    out_ref[...] = x_ref[...] + _dot(o_ref[...], w_ref[...])


def _linres(x2, o2, w, lead):
    t, d = x2.shape
    assert t % LIN_TM == 0
    return pl.pallas_call(
        _linres_kernel,
        grid=(t // LIN_TM,),
        in_specs=[
            pl.BlockSpec((LIN_TM, d), lambda i: (i, 0)),
            pl.BlockSpec((LIN_TM, d), lambda i: (i, 0)),
            _layer_spec((d, d), lead),
        ],
        out_specs=pl.BlockSpec((LIN_TM, d), lambda i: (i, 0)),
        out_shape=jax.ShapeDtypeStruct((t, d), F32),
        compiler_params=_params("parallel"),
        name="linres",
    )(x2, o2, w)


RG_TS = 512


def _rg_kernel(x_ref, g_ref, win_ref, cw_ref, cb_ref, wa_ref, ba_ref, wx_ref, bx_ref, lam_ref, wout_ref,
               o_ref, ext_scr, a_scr, b_scr, hcar_scr, *, ts, d, conv_w, nblk, blk):
    s = pl.program_id(1)
    pad = SUBLANES

    @pl.when(s == 0)
    def _():
        ext_scr[0:pad, :] = jnp.zeros((pad, d), F32)
        hcar_scr[...] = jnp.zeros_like(hcar_scr)

    x = x_ref[...]
    h = _rms(x, g_ref[...]).astype(BF16)
    gate = _dot(h, win_ref[:, 0:d])
    u = _dot(h, win_ref[:, d:2 * d])

    ext_scr[pad:pad + ts, :] = u
    uc = u * cw_ref[conv_w - 1:conv_w, :] + cb_ref[...]
    for k in range(conv_w - 1):
        off = pad - (conv_w - 1) + k
        uc = uc + ext_scr[off:off + ts, :] * cw_ref[k:k + 1, :]
    ext_scr[0:pad, :] = u[ts - pad:ts, :]

    ub = uc.astype(BF16)
    ra = jnp.concatenate([_dot(ub[:, n * blk:(n + 1) * blk], wa_ref[n]) for n in range(nblk)], axis=-1)
    rx = jnp.concatenate([_dot(ub[:, n * blk:(n + 1) * blk], wx_ref[n]) for n in range(nblk)], axis=-1)
    r = _sigmoid(ra + ba_ref[...])
    ig = _sigmoid(rx + bx_ref[...])
    log_a = (-RG_C) * r * _softplus(-lam_ref[...])
    a = jnp.exp(log_a)
    a_scr[...] = a
    b_scr[...] = jnp.sqrt(-jnp.tanh(log_a) * (a * a + 1.0)) * (ig * uc)

    row = lax.broadcasted_iota(jnp.int32, (SUBLANES, d), 0)

    def group(gi, carry):
        r0 = pl.multiple_of(gi * SUBLANES, SUBLANES)
        a = a_scr[pl.ds(r0, SUBLANES), :]
        b = b_scr[pl.ds(r0, SUBLANES), :]
        for sh in (1, 2, 4):
            keep = row >= sh
            a_s = jnp.where(keep, pltpu.roll(a, sh, axis=0), 1.0)
            b_s = jnp.where(keep, pltpu.roll(b, sh, axis=0), 0.0)
            b = a * b_s + b
            a = a * a_s
        hg = a * carry + b
        b_scr[pl.ds(r0, SUBLANES), :] = hg
        return hg[SUBLANES - 1:SUBLANES, :]

    hcar_scr[...] = lax.fori_loop(0, ts // SUBLANES, group, hcar_scr[...], unroll=4)

    y = (b_scr[...] * jax.nn.gelu(gate)).astype(BF16)
    o_ref[...] = x + _dot(y, wout_ref[...])


def _rglru(x, g, w_in, conv_w, conv_b, w_a, b_a, w_x, b_x, lam, w_out, lead):
    bsz, seq, d = x.shape
    kw = conv_w.shape[-2]
    nblk, blk = w_a.shape[-3], w_a.shape[-1]
    ts = RG_TS
    assert seq % ts == 0 and kw - 1 <= SUBLANES and nblk * blk == d
    return pl.pallas_call(
        functools.partial(_rg_kernel, ts=ts, d=d, conv_w=kw, nblk=nblk, blk=blk),
        grid=(bsz, seq // ts),
        in_specs=[
            pl.BlockSpec((None, ts, d), lambda b, s: (b, s, 0)),
            _const_spec((1, d)),
            _layer_spec((d, 2 * d), lead),
            _const_spec((kw, d)),
            _const_spec((1, d)),
            _layer_spec((nblk, blk, blk), lead),
            _const_spec((1, d)),
            _layer_spec((nblk, blk, blk), lead),
            _const_spec((1, d)),
            _const_spec((1, d)),
            _layer_spec((d, d), lead),
        ],
        out_specs=pl.BlockSpec((None, ts, d), lambda b, s: (b, s, 0)),
        out_shape=jax.ShapeDtypeStruct((bsz, seq, d), F32),
        scratch_shapes=[
            pltpu.VMEM((ts + SUBLANES, d), F32),
            pltpu.VMEM((ts, d), F32),
            pltpu.VMEM((ts, d), F32),
            pltpu.VMEM((1, d), F32),
        ],
        compiler_params=_params("parallel", "arbitrary"),
        name="rglru",
    )(x, g, w_in, conv_w, conv_b, w_a, b_a, w_x, b_x, lam, w_out)


CV_TS = 256
CV_ROWS = 64
CV_HIST = 32


def _cv_kernel(x_ref, g_ref, win_ref, bin_ref, dw_ref, dwb_ref, lng_ref, lnb_ref, wout_ref, bout_ref,
               o_ref, ext_scr, y_scr, *, ts, d, kw):
    s = pl.program_id(1)

    @pl.when(s == 0)
    def _():
        ext_scr[0:CV_HIST, :] = jnp.zeros((CV_HIST, d), F32)

    x = x_ref[...]
    h = _rms(x, g_ref[...]).astype(BF16)
    val = _dot(h, win_ref[:, 0:d]) + bin_ref[:, 0:d]
    gate = _dot(h, win_ref[:, d:2 * d]) + bin_ref[:, d:2 * d]
    hh = val * _sigmoid(gate)
    ext_scr[CV_HIST:CV_HIST + ts, :] = hh

    def col_block(c, carry):
        c0 = pl.multiple_of(c * LANES, LANES)
        w = dw_ref[:, pl.ds(c0, LANES)]
        bias = dwb_ref[:, pl.ds(c0, LANES)]
        for r0 in range(0, ts, CV_ROWS):
            acc = jnp.zeros((CV_ROWS, LANES), F32) + bias
            for k in range(kw):
                off = CV_HIST + r0 - (kw - 1) + k
                acc = acc + ext_scr[off:off + CV_ROWS, pl.ds(c0, LANES)] * w[k:k + 1, :]
            y_scr[r0:r0 + CV_ROWS, pl.ds(c0, LANES)] = acc
        return carry

    lax.fori_loop(0, d // LANES, col_block, 0)
    ext_scr[0:CV_HIST, :] = hh[ts - CV_HIST:ts, :]

    y = y_scr[...]
    mu = jnp.mean(y, axis=-1, keepdims=True)
    yc = y - mu
    var = jnp.mean(yc * yc, axis=-1, keepdims=True)
    z = yc * lax.rsqrt(var + EPS) * lng_ref[...] + lnb_ref[...]
    z = (z * _sigmoid(z)).astype(BF16)
    o_ref[...] = x + _dot(z, wout_ref[...]) + bout_ref[...]


def _conv_module(x, g, w_in, b_in, dw_w, dw_b, ln_g, ln_b, w_out, b_out, lead):
    bsz, seq, d = x.shape
    kw = dw_w.shape[-2]
    ts = CV_TS
    assert seq % ts == 0 and kw - 1 <= CV_HIST and ts % CV_ROWS == 0
    return pl.pallas_call(
        functools.partial(_cv_kernel, ts=ts, d=d, kw=kw),
        grid=(bsz, seq // ts),
        in_specs=[
            pl.BlockSpec((None, ts, d), lambda b, s: (b, s, 0)),
            _const_spec((1, d)),
            _layer_spec((d, 2 * d), lead),
            _const_spec((1, 2 * d)),
            _const_spec((kw, d)),
            _const_spec((1, d)),
            _const_spec((1, d)),
            _const_spec((1, d)),
            _layer_spec((d, d), lead),
            _const_spec((1, d)),
        ],
        out_specs=pl.BlockSpec((None, ts, d), lambda b, s: (b, s, 0)),
        out_shape=jax.ShapeDtypeStruct((bsz, seq, d), F32),
        scratch_shapes=[
            pltpu.VMEM((ts + CV_HIST, d), F32),
            pltpu.VMEM((ts, d), F32),
        ],
        compiler_params=_params("parallel", "arbitrary"),
        name="convmod",
    )(x, g, w_in, b_in, dw_w, dw_b, ln_g, ln_b, w_out, b_out)


FX_TS = 512
FX_TQ = 256


def _head_rms(t, g2, lane_lo, hd):
    sq = t * t
    s_lo = jnp.sum(jnp.where(lane_lo, sq, 0.0), axis=-1, keepdims=True)
    s_hi = jnp.sum(jnp.where(lane_lo, 0.0, sq), axis=-1, keepdims=True)
    ms = jnp.where(lane_lo, s_lo, s_hi) * (1.0 / hd)
    return t * lax.rsqrt(ms + EPS) * g2


def _fxproj_kernel(x_ref, g_ref, wqkv_ref, wf_ref, bf_ref, qg_ref, kg_ref,
                   q_ref, k_ref, v_ref, cum_ref, car_scr, *, ts, d, hd, nh):
    s = pl.program_id(1)

    @pl.when(s == 0)
    def _():
        car_scr[...] = jnp.zeros_like(car_scr)

    x = x_ref[...]
    h = _rms(x, g_ref[...]).astype(BF16)
    lane_lo = lax.broadcasted_iota(jnp.int32, (ts, LANES), 1) < hd
    scale = hd ** -0.5
    q = _dot(h, wqkv_ref[:, 0:d])
    k = _dot(h, wqkv_ref[:, d:2 * d])
    for c in range(d // LANES):
        cols = slice(c * LANES, (c + 1) * LANES)
        q_ref[:, cols] = (_head_rms(q[:, cols], qg_ref[...], lane_lo, hd) * scale).astype(BF16)
        k_ref[:, cols] = _head_rms(k[:, cols], kg_ref[...], lane_lo, hd).astype(BF16)
    v_ref[...] = _dot(h, wqkv_ref[:, 2 * d:3 * d]).astype(BF16)

    f = _dot(h, wf_ref[...]) + bf_ref[...]
    cum = jnp.minimum(f, 0.0) - jnp.log1p(jnp.exp(-jnp.abs(f)))
    row = lax.broadcasted_iota(jnp.int32, (ts, LANES), 0)
    sh = 1
    while sh < ts:
        cum = cum + jnp.where(row >= sh, pltpu.roll(cum, sh, axis=0), 0.0)
        sh *= 2
    cum = cum + car_scr[...]
    car_scr[...] = cum[ts - 1:ts, :]
    cum_ref[...] = jnp.transpose(cum)[0:nh, :]


def _fox_proj(x, g, w_qkv, w_f, b_f, q_g2, k_g2, lead, nh):
    bsz, seq, d = x.shape
    hd = d // nh
    ts = FX_TS
    assert seq % ts == 0 and 2 * hd == LANES and nh % SUBLANES == 0
    tile = pl.BlockSpec((None, ts, d), lambda b, s: (b, s, 0))
    return pl.pallas_call(
        functools.partial(_fxproj_kernel, ts=ts, d=d, hd=hd, nh=nh),
        grid=(bsz, seq // ts),
        in_specs=[
            tile,
            _const_spec((1, d)),
            _layer_spec((d, 3 * d), lead),
            _layer_spec((d, LANES), lead),
            _const_spec((1, LANES)),
            _const_spec((1, LANES)),
            _const_spec((1, LANES)),
        ],
        out_specs=[tile, tile, tile, pl.BlockSpec((None, nh, ts), lambda b, s: (b, 0, s))],
        out_shape=[
            jax.ShapeDtypeStruct((bsz, seq, d), BF16),
            jax.ShapeDtypeStruct((bsz, seq, d), BF16),
            jax.ShapeDtypeStruct((bsz, seq, d), BF16),
            jax.ShapeDtypeStruct((bsz, nh, seq), F32),
        ],
        scratch_shapes=[pltpu.VMEM((1, LANES), F32)],
        compiler_params=_params("parallel", "arbitrary"),
        name="foxproj",
    )(x, g, w_qkv, w_f, b_f, q_g2, k_g2)


def _fxattn_kernel(q_ref, k_ref, v_ref, cum_ref, o_ref, m_scr, l_scr, acc_scr, *, tq, hd):
    i = pl.program_id(2)
    q = q_ref[...]
    lane_lo = lax.broadcasted_iota(jnp.int32, (tq, LANES), 1) < hd
    zero = jnp.zeros_like(q)
    qs = (jnp.where(lane_lo, q, zero), jnp.where(lane_lo, zero, q))

    def scores(hh, j0):
        kb = k_ref[pl.ds(j0, tq), :]
        return _dot_nt(qs[hh], kb) - cum_ref[hh:hh + 1, pl.ds(j0, tq)]

    d0 = pl.multiple_of(i * tq, tq)
    causal = (lax.broadcasted_iota(jnp.int32, (tq, tq), 0) >= lax.broadcasted_iota(jnp.int32, (tq, tq), 1))
    vb = v_ref[pl.ds(d0, tq), :]
    for hh in range(2):
        sc = jnp.where(causal, scores(hh, d0), NEG_BIG)
        m = jnp.max(sc, axis=-1, keepdims=True)
        p = jnp.exp(sc - m)
        m_scr[hh] = m
        l_scr[hh] = jnp.sum(p, axis=-1, keepdims=True)
        acc_scr[hh] = _dot(p.astype(BF16), vb)

    def kv_block(j, carry):
        j0 = pl.multiple_of(j * tq, tq)
        vb = v_ref[pl.ds(j0, tq), :]
        for hh in range(2):
            sc = scores(hh, j0)
            m_old = m_scr[hh]
            m_new = jnp.maximum(m_old, jnp.max(sc, axis=-1, keepdims=True))
            alpha = jnp.exp(m_old - m_new)
            p = jnp.exp(sc - m_new)
            m_scr[hh] = m_new
            l_scr[hh] = alpha * l_scr[hh] + jnp.sum(p, axis=-1, keepdims=True)
            acc_scr[hh] = alpha * acc_scr[hh] + _dot(p.astype(BF16), vb)
        return carry

    lax.fori_loop(0, i, kv_block, 0)
    o_lo = acc_scr[0] / l_scr[0]
    o_hi = acc_scr[1] / l_scr[1]
    o_ref[...] = jnp.where(lane_lo, o_lo, o_hi).astype(BF16)


def _fox_attn(q, k, v, cum4, nh):
    bsz, seq, d = q.shape
    hd = d // nh
    tq = FX_TQ
    assert seq % tq == 0 and 2 * hd == LANES
    return pl.pallas_call(
        functools.partial(_fxattn_kernel, tq=tq, hd=hd),
        grid=(bsz, nh // 2, seq // tq),
        in_specs=[
            pl.BlockSpec((None, tq, LANES), lambda b, p, i: (b, i, p)),
            pl.BlockSpec((None, seq, LANES), lambda b, p, i: (b, 0, p)),
            pl.BlockSpec((None, seq, LANES), lambda b, p, i: (b, 0, p)),
            pl.BlockSpec((None, None, 2, seq), lambda b, p, i: (b, p, 0, 0)),
        ],
        out_specs=pl.BlockSpec((None, tq, LANES), lambda b, p, i: (b, i, p)),
        out_shape=jax.ShapeDtypeStruct((bsz, seq, d), BF16),
        scratch_shapes=[
            pltpu.VMEM((2, tq, 1), F32),
            pltpu.VMEM((2, tq, 1), F32),
            pltpu.VMEM((2, tq, LANES), F32),
        ],
        compiler_params=_params("parallel", "parallel", "arbitrary"),
        name="foxattn",
    )(q, k, v, cum4)


def kernel(x, ffn_norm, ffn_w_in, ffn_w_out, mix_norm, rg_w_in, rg_conv_w, rg_conv_b, rg_w_a, rg_b_a, rg_w_x,
           rg_b_x, rg_lambda, rg_w_out, fox_w_in, fox_b_f, fox_q_norm, fox_k_norm, fox_w_out, cv_w_in, cv_b_in,
           cv_dw_w, cv_dw_b, cv_ln_g, cv_ln_b, cv_w_out, cv_b_out):
    bsz, seq, d = x.shape
    depth = ffn_norm.shape[0]
    nh = fox_b_f.shape[-1]
    hd = fox_q_norm.shape[-1]
    assert nh * hd == d and nh <= LANES

    ffn_w_in_b = ffn_w_in.astype(BF16)
    ffn_w_out_b = ffn_w_out.astype(BF16)
    rg_w_in_b, rg_w_a_b, rg_w_x_b, rg_w_out_b = (w.astype(BF16) for w in (rg_w_in, rg_w_a, rg_w_x, rg_w_out))
    fox_w_qkv_b = fox_w_in[:, :, :3 * d].astype(BF16)
    fox_w_f_b = jnp.pad(fox_w_in[:, :, 3 * d:], ((0, 0), (0, 0), (0, LANES - nh))).astype(BF16)
    fox_w_out_b = fox_w_out.astype(BF16)
    cv_w_in_b, cv_w_out_b = cv_w_in.astype(BF16), cv_w_out.astype(BF16)

    def row(p):
        return p[None, :]

    def pad_lanes(p):
        return jnp.pad(p, (0, LANES - p.shape[0]))[None, :]

    def twice(p):
        return jnp.concatenate([p, p])[None, :]

    for i in range(depth):
        kind, j = i % N_MIXERS, i // N_MIXERS
        x = _ffn(x.reshape(bsz * seq, d), row(ffn_norm[i, 0]), ffn_w_in_b, ffn_w_out_b, (i, 0))
        x = x.reshape(bsz, seq, d)
        g = row(mix_norm[i])
        if kind == 0:
            x = _rglru(x, g, rg_w_in_b, rg_conv_w[j], row(rg_conv_b[j]), rg_w_a_b, row(rg_b_a[j]), rg_w_x_b,
                       row(rg_b_x[j]), row(rg_lambda[j]), rg_w_out_b, (j,))
        elif kind == 1:
            q, k, v, cum = _fox_proj(x, g, fox_w_qkv_b, fox_w_f_b, pad_lanes(fox_b_f[j]), twice(fox_q_norm[j]),
                                     twice(fox_k_norm[j]), (j,), nh)
            o = _fox_attn(q, k, v, cum.reshape(bsz, nh // 2, 2, seq), nh)
            x = _linres(x.reshape(bsz * seq, d), o.reshape(bsz * seq, d), fox_w_out_b, (j,))
            x = x.reshape(bsz, seq, d)
        else:
            x = _conv_module(x, g, cv_w_in_b, row(cv_b_in[j]), cv_dw_w[j], row(cv_dw_b[j]), row(cv_ln_g[j]),
                             row(cv_ln_b[j]), cv_w_out_b, row(cv_b_out[j]), (j,))
        x = _ffn(x.reshape(bsz * seq, d), row(ffn_norm[i, 1]), ffn_w_in_b, ffn_w_out_b, (i, 1))
        x = x.reshape(bsz, seq, d)
    return x
```

```python
import functools

import jax
import jax.numpy as jnp
from jax import lax
from jax.experimental import pallas as pl
from jax.experimental.pallas import tpu as pltpu

F32 = jnp.float32
BF16 = jnp.bfloat16

EPS = 1e-6
N_MIXERS = 3
RG_C = 8.0

LANES = 128
SUBLANES = 8
VMEM_LIMIT_BYTES = 56 * 1024 * 1024

NEG_BIG = -1e30
LOG2E = 1.4426950408889634


def _dot(a, b):
    return jnp.dot(a, b, preferred_element_type=F32)


def _dot_nt(a, b):
    return lax.dot_general(a, b, (((1,), (1,)), ((), ())), preferred_element_type=F32)


def _sigmoid(x):
    return 1.0 / (1.0 + jnp.exp(-x))


def _rms(x, g):
    ms = jnp.mean(x * x, axis=-1, keepdims=True)
    return x * lax.rsqrt(ms + EPS) * g


def _softplus(z):
    return jnp.maximum(z, 0.0) + jnp.log1p(jnp.exp(-jnp.abs(z)))


def _params(*semantics):
    return pltpu.CompilerParams(dimension_semantics=semantics, vmem_limit_bytes=VMEM_LIMIT_BYTES)


def _const_spec(shape):
    n = len(shape)
    return pl.BlockSpec(shape, lambda *_: (0,) * n, pipeline_mode=pl.Buffered(1))


def _layer_spec(shape, lead):
    n = len(shape)
    block = (None,) * len(lead) + tuple(shape)
    return pl.BlockSpec(block, lambda *_: tuple(lead) + (0,) * n, pipeline_mode=pl.Buffered(1))


FFN_TM = 512
FFN_TF = 256


def _ffn_kernel(x_ref, g_ref, win_ref, wout_ref, o_ref, *, d_ff, tf):
    x = x_ref[...]
    h = _rms(x, g_ref[...]).astype(BF16)
    acc = None
    for c in range(d_ff // tf):
        gate = _dot(h, win_ref[:, c * tf:(c + 1) * tf])
        up = _dot(h, win_ref[:, d_ff + c * tf:d_ff + (c + 1) * tf])
        a = (gate * _sigmoid(gate) * up).astype(BF16)
        part = _dot(a, wout_ref[c * tf:(c + 1) * tf, :])
        acc = part if acc is None else acc + part
    o_ref[...] = x + 0.5 * acc


def _ffn(x2, g, w_in, w_out, lead):
    t, d = x2.shape
    d_ff = w_out.shape[-2]
    assert t % FFN_TM == 0 and d_ff % FFN_TF == 0
    return pl.pallas_call(
        functools.partial(_ffn_kernel, d_ff=d_ff, tf=FFN_TF),
        grid=(t // FFN_TM,),
        in_specs=[
            pl.BlockSpec((FFN_TM, d), lambda i: (i, 0)),
            _const_spec((1, d)),
            _layer_spec((d, 2 * d_ff), lead),
            _layer_spec((d_ff, d), lead),
        ],
        out_specs=pl.BlockSpec((FFN_TM, d), lambda i: (i, 0)),
        out_shape=jax.ShapeDtypeStruct((t, d), F32),
        compiler_params=_params("parallel"),
        name="ffn",
    )(x2, g, w_in, w_out)


LIN_TM = 512


def _linres_kernel(x_ref, o_ref, w_ref, out_ref):
    out_ref[...] = x_ref[...] + _dot(o_ref[...], w_ref[...])


def _linres(x2, o2, w, lead):
    t, d = x2.shape
    assert t % LIN_TM == 0
    return pl.pallas_call(
        _linres_kernel,
        grid=(t // LIN_TM,),
        in_specs=[
            pl.BlockSpec((LIN_TM, d), lambda i: (i, 0)),
            pl.BlockSpec((LIN_TM, d), lambda i: (i, 0)),
            _layer_spec((d, d), lead),
        ],
        out_specs=pl.BlockSpec((LIN_TM, d), lambda i: (i, 0)),
        out_shape=jax.ShapeDtypeStruct((t, d), F32),
        compiler_params=_params("parallel"),
        name="linres",
    )(x2, o2, w)


RG_TS = 512


def _rg_kernel(x_ref, g_ref, win_ref, cw_ref, cb_ref, wa_ref, ba_ref, wx_ref, bx_ref, lam_ref, wout_ref,
               o_ref, ext_scr, a_scr, b_scr, hcar_scr, *, ts, d, conv_w, nblk, blk):
    s = pl.program_id(1)
    pad = SUBLANES

    @pl.when(s == 0)
    def _():
        ext_scr[0:pad, :] = jnp.zeros((pad, d), F32)
        hcar_scr[...] = jnp.zeros_like(hcar_scr)

    x = x_ref[...]
    h = _rms(x, g_ref[...]).astype(BF16)
    gate = _dot(h, win_ref[:, 0:d])
    u = _dot(h, win_ref[:, d:2 * d])

    ext_scr[pad:pad + ts, :] = u
    uc = u * cw_ref[conv_w - 1:conv_w, :] + cb_ref[...]
    for k in range(conv_w - 1):
        off = pad - (conv_w - 1) + k
        uc = uc + ext_scr[off:off + ts, :] * cw_ref[k:k + 1, :]
    ext_scr[0:pad, :] = u[ts - pad:ts, :]

    ub = uc.astype(BF16)
    ra = jnp.concatenate([_dot(ub[:, n * blk:(n + 1) * blk], wa_ref[n]) for n in range(nblk)], axis=-1)
    rx = jnp.concatenate([_dot(ub[:, n * blk:(n + 1) * blk], wx_ref[n]) for n in range(nblk)], axis=-1)
    r = _sigmoid(ra + ba_ref[...])
    ig = _sigmoid(rx + bx_ref[...])
    log_a = (-RG_C) * r * _softplus(-lam_ref[...])
    a = jnp.exp(log_a)
    a_scr[...] = a
    b_scr[...] = jnp.sqrt(-jnp.tanh(log_a) * (a * a + 1.0)) * (ig * uc)

    row = lax.broadcasted_iota(jnp.int32, (SUBLANES, d), 0)

    def group(gi, carry):
        r0 = pl.multiple_of(gi * SUBLANES, SUBLANES)
        a = a_scr[pl.ds(r0, SUBLANES), :]
        b = b_scr[pl.ds(r0, SUBLANES), :]
        for sh in (1, 2, 4):
            keep = row >= sh
            a_s = jnp.where(keep, pltpu.roll(a, sh, axis=0), 1.0)
            b_s = jnp.where(keep, pltpu.roll(b, sh, axis=0), 0.0)
            b = a * b_s + b
            a = a * a_s
        hg = a * carry + b
        b_scr[pl.ds(r0, SUBLANES), :] = hg
        return hg[SUBLANES - 1:SUBLANES, :]

    hcar_scr[...] = lax.fori_loop(0, ts // SUBLANES, group, hcar_scr[...], unroll=4)

    y = (b_scr[...] * jax.nn.gelu(gate)).astype(BF16)
    o_ref[...] = x + _dot(y, wout_ref[...])


def _rglru(x, g, w_in, conv_w, conv_b, w_a, b_a, w_x, b_x, lam, w_out, lead):
    bsz, seq, d = x.shape
    kw = conv_w.shape[-2]
    nblk, blk = w_a.shape[-3], w_a.shape[-1]
    ts = RG_TS
    assert seq % ts == 0 and kw - 1 <= SUBLANES and nblk * blk == d
    return pl.pallas_call(
        functools.partial(_rg_kernel, ts=ts, d=d, conv_w=kw, nblk=nblk, blk=blk),
        grid=(bsz, seq // ts),
        in_specs=[
            pl.BlockSpec((None, ts, d), lambda b, s: (b, s, 0)),
            _const_spec((1, d)),
            _layer_spec((d, 2 * d), lead),
            _const_spec((kw, d)),
            _const_spec((1, d)),
            _layer_spec((nblk, blk, blk), lead),
            _const_spec((1, d)),
            _layer_spec((nblk, blk, blk), lead),
            _const_spec((1, d)),
            _const_spec((1, d)),
            _layer_spec((d, d), lead),
        ],
        out_specs=pl.BlockSpec((None, ts, d), lambda b, s: (b, s, 0)),
        out_shape=jax.ShapeDtypeStruct((bsz, seq, d), F32),
        scratch_shapes=[
            pltpu.VMEM((ts + SUBLANES, d), F32),
            pltpu.VMEM((ts, d), F32),
            pltpu.VMEM((ts, d), F32),
            pltpu.VMEM((1, d), F32),
        ],
        compiler_params=_params("parallel", "arbitrary"),
        name="rglru",
    )(x, g, w_in, conv_w, conv_b, w_a, b_a, w_x, b_x, lam, w_out)


CV_TS = 256
CV_ROWS = 64
CV_HIST = 32


def _cv_kernel(x_ref, g_ref, win_ref, bin_ref, dw_ref, dwb_ref, lng_ref, lnb_ref, wout_ref, bout_ref,
               o_ref, ext_scr, y_scr, *, ts, d, kw):
    s = pl.program_id(1)

    @pl.when(s == 0)
    def _():
        ext_scr[0:CV_HIST, :] = jnp.zeros((CV_HIST, d), F32)

    x = x_ref[...]
    h = _rms(x, g_ref[...]).astype(BF16)
    val = _dot(h, win_ref[:, 0:d]) + bin_ref[:, 0:d]
    gate = _dot(h, win_ref[:, d:2 * d]) + bin_ref[:, d:2 * d]
    hh = val * _sigmoid(gate)
    ext_scr[CV_HIST:CV_HIST + ts, :] = hh

    def col_block(c, carry):
        c0 = pl.multiple_of(c * LANES, LANES)
        w = dw_ref[:, pl.ds(c0, LANES)]
        bias = dwb_ref[:, pl.ds(c0, LANES)]
        for r0 in range(0, ts, CV_ROWS):
            acc = jnp.zeros((CV_ROWS, LANES), F32) + bias
            for k in range(kw):
                off = CV_HIST + r0 - (kw - 1) + k
                acc = acc + ext_scr[off:off + CV_ROWS, pl.ds(c0, LANES)] * w[k:k + 1, :]
            y_scr[r0:r0 + CV_ROWS, pl.ds(c0, LANES)] = acc
        return carry

    lax.fori_loop(0, d // LANES, col_block, 0)
    ext_scr[0:CV_HIST, :] = hh[ts - CV_HIST:ts, :]

    y = y_scr[...]
    mu = jnp.mean(y, axis=-1, keepdims=True)
    yc = y - mu
    var = jnp.mean(yc * yc, axis=-1, keepdims=True)
    z = yc * lax.rsqrt(var + EPS) * lng_ref[...] + lnb_ref[...]
    z = (z * _sigmoid(z)).astype(BF16)
    o_ref[...] = x + _dot(z, wout_ref[...]) + bout_ref[...]


def _conv_module(x, g, w_in, b_in, dw_w, dw_b, ln_g, ln_b, w_out, b_out, lead):
    bsz, seq, d = x.shape
    kw = dw_w.shape[-2]
    ts = CV_TS
    assert seq % ts == 0 and kw - 1 <= CV_HIST and ts % CV_ROWS == 0
    return pl.pallas_call(
        functools.partial(_cv_kernel, ts=ts, d=d, kw=kw),
        grid=(bsz, seq // ts),
        in_specs=[
            pl.BlockSpec((None, ts, d), lambda b, s: (b, s, 0)),
            _const_spec((1, d)),
            _layer_spec((d, 2 * d), lead),
            _const_spec((1, 2 * d)),
            _const_spec((kw, d)),
            _const_spec((1, d)),
            _const_spec((1, d)),
            _const_spec((1, d)),
            _layer_spec((d, d), lead),
            _const_spec((1, d)),
        ],
        out_specs=pl.BlockSpec((None, ts, d), lambda b, s: (b, s, 0)),
        out_shape=jax.ShapeDtypeStruct((bsz, seq, d), F32),
        scratch_shapes=[
            pltpu.VMEM((ts + CV_HIST, d), F32),
            pltpu.VMEM((ts, d), F32),
        ],
        compiler_params=_params("parallel", "arbitrary"),
        name="convmod",
    )(x, g, w_in, b_in, dw_w, dw_b, ln_g, ln_b, w_out, b_out)


FX_TS = 512
FX_TQ = 512


def _head_rms(t, g2, lane_lo, hd):
    sq = t * t
    s_lo = jnp.sum(jnp.where(lane_lo, sq, 0.0), axis=-1, keepdims=True)
    s_hi = jnp.sum(jnp.where(lane_lo, 0.0, sq), axis=-1, keepdims=True)
    ms = jnp.where(lane_lo, s_lo, s_hi) * (1.0 / hd)
    return t * lax.rsqrt(ms + EPS) * g2


def _fxproj_kernel(x_ref, g_ref, wqkv_ref, wf_ref, bf_ref, qg_ref, kg_ref,
                   q_ref, kt_ref, v_ref, cum_ref, car_scr, *, ts, d, hd, nh):
    s = pl.program_id(1)

    @pl.when(s == 0)
    def _():
        car_scr[...] = jnp.zeros_like(car_scr)

    x = x_ref[...]
    h = _rms(x, g_ref[...]).astype(BF16)
    lane_lo = lax.broadcasted_iota(jnp.int32, (ts, LANES), 1) < hd
    scale = hd ** -0.5 * LOG2E
    q = _dot(h, wqkv_ref[:, 0:d])
    k = _dot(h, wqkv_ref[:, d:2 * d])
    for c in range(d // LANES):
        cols = slice(c * LANES, (c + 1) * LANES)
        q_ref[:, cols] = (_head_rms(q[:, cols], qg_ref[...], lane_lo, hd) * scale).astype(BF16)
        kt_ref[cols, :] = jnp.transpose(_head_rms(k[:, cols], kg_ref[...], lane_lo, hd)).astype(BF16)
    v_ref[...] = _dot(h, wqkv_ref[:, 2 * d:3 * d]).astype(BF16)

    f = _dot(h, wf_ref[...]) + bf_ref[...]
    cum = jnp.minimum(f, 0.0) - jnp.log1p(jnp.exp(-jnp.abs(f)))
    row = lax.broadcasted_iota(jnp.int32, (ts, LANES), 0)
    sh = 1
    while sh < ts:
        cum = cum + jnp.where(row >= sh, pltpu.roll(cum, sh, axis=0), 0.0)
        sh *= 2
    cum = cum + car_scr[...]
    car_scr[...] = cum[ts - 1:ts, :]
    cum_ref[...] = jnp.transpose(cum * LOG2E)[0:nh, :]


def _fox_proj(x, g, w_qkv, w_f, b_f, q_g2, k_g2, lead, nh):
    bsz, seq, d = x.shape
    hd = d // nh
    ts = FX_TS
    assert seq % ts == 0 and 2 * hd == LANES and nh % SUBLANES == 0
    tile = pl.BlockSpec((None, ts, d), lambda b, s: (b, s, 0))
    return pl.pallas_call(
        functools.partial(_fxproj_kernel, ts=ts, d=d, hd=hd, nh=nh),
        grid=(bsz, seq // ts),
        in_specs=[
            tile,
            _const_spec((1, d)),
            _layer_spec((d, 3 * d), lead),
            _layer_spec((d, LANES), lead),
            _const_spec((1, LANES)),
            _const_spec((1, LANES)),
            _const_spec((1, LANES)),
        ],
        out_specs=[tile, pl.BlockSpec((None, d, ts), lambda b, s: (b, 0, s)), tile,
                   pl.BlockSpec((None, nh, ts), lambda b, s: (b, 0, s))],
        out_shape=[
            jax.ShapeDtypeStruct((bsz, seq, d), BF16),
            jax.ShapeDtypeStruct((bsz, d, seq), BF16),
            jax.ShapeDtypeStruct((bsz, seq, d), BF16),
            jax.ShapeDtypeStruct((bsz, nh, seq), F32),
        ],
        scratch_shapes=[pltpu.VMEM((1, LANES), F32)],
        compiler_params=_params("parallel", "arbitrary"),
        name="foxproj",
    )(x, g, w_qkv, w_f, b_f, q_g2, k_g2)


def _fxattn_kernel(q_ref, kt_ref, v_ref, cum_ref, o_ref, m0_scr, m1_scr, acc0_scr, acc1_scr, *, tq, hd):
    i = pl.program_id(2)
    q = q_ref[...]
    lane_lo = lax.broadcasted_iota(jnp.int32, (tq, LANES), 1) < hd
    zero = jnp.zeros_like(q)
    qs = (jnp.where(lane_lo, q, zero), jnp.where(lane_lo, zero, q))
    m_scrs = (m0_scr, m1_scr)
    acc_scrs = (acc0_scr, acc1_scr)
    reps = tq // LANES

    def scores(hh, j0):
        return _dot(qs[hh], kt_ref[:, pl.ds(j0, tq)]) - cum_ref[hh:hh + 1, pl.ds(j0, tq)]

    def values(j0):
        vb = v_ref[pl.ds(j0, tq), :]
        one = jnp.ones_like(vb)
        return (jnp.where(lane_lo, vb, one), jnp.where(lane_lo, one, vb))

    def row_max(sc):
        return jnp.broadcast_to(jnp.max(sc, axis=-1, keepdims=True), (tq, LANES))

    d0 = pl.multiple_of(i * tq, tq)
    causal = (lax.broadcasted_iota(jnp.int32, (tq, tq), 0) >= lax.broadcasted_iota(jnp.int32, (tq, tq), 1))
    vs = values(d0)
    for hh in range(2):
        sc = jnp.where(causal, scores(hh, d0), NEG_BIG)
        m = row_max(sc)
        m_scrs[hh][...] = m
        acc_scrs[hh][...] = _dot(jnp.exp2(sc - pltpu.repeat(m, reps, axis=1)).astype(BF16), vs[hh])

    def kv_block(j, carry):
        j0 = pl.multiple_of(j * tq, tq)
        vs = values(j0)
        for hh in range(2):
            sc = scores(hh, j0)
            m_old = m_scrs[hh][...]
            m_new = jnp.maximum(m_old, row_max(sc))
            m_scrs[hh][...] = m_new
            p = jnp.exp2(sc - pltpu.repeat(m_new, reps, axis=1)).astype(BF16)
            acc_scrs[hh][...] = jnp.exp2(m_old - m_new) * acc_scrs[hh][...] + _dot(p, vs[hh])
        return carry

    lax.fori_loop(0, i, kv_block, 0)
    acc_lo, acc_hi = acc0_scr[...], acc1_scr[...]
    o_lo = acc_lo / acc_lo[:, LANES - 1:LANES]
    o_hi = acc_hi / acc_hi[:, 0:1]
    o_ref[...] = jnp.where(lane_lo, o_lo, o_hi).astype(BF16)


def _fox_attn(q, kt, v, cum4, nh):
    bsz, seq, d = q.shape
    hd = d // nh
    tq = FX_TQ
    assert seq % tq == 0 and 2 * hd == LANES
    return pl.pallas_call(
        functools.partial(_fxattn_kernel, tq=tq, hd=hd),
        grid=(bsz, nh // 2, seq // tq),
        in_specs=[
            pl.BlockSpec((None, tq, LANES), lambda b, p, i: (b, i, p)),
            pl.BlockSpec((None, LANES, seq), lambda b, p, i: (b, p, 0)),
            pl.BlockSpec((None, seq, LANES), lambda b, p, i: (b, 0, p)),
            pl.BlockSpec((None, None, 2, seq), lambda b, p, i: (b, p, 0, 0)),
        ],
        out_specs=pl.BlockSpec((None, tq, LANES), lambda b, p, i: (b, i, p)),
        out_shape=jax.ShapeDtypeStruct((bsz, seq, d), BF16),
        scratch_shapes=[pltpu.VMEM((tq, LANES), F32)] * 4,
        compiler_params=_params("parallel", "parallel", "arbitrary"),
        name="foxattn",
    )(q, kt, v, cum4)


def kernel(x, ffn_norm, ffn_w_in, ffn_w_out, mix_norm, rg_w_in, rg_conv_w, rg_conv_b, rg_w_a, rg_b_a, rg_w_x,
           rg_b_x, rg_lambda, rg_w_out, fox_w_in, fox_b_f, fox_q_norm, fox_k_norm, fox_w_out, cv_w_in, cv_b_in,
           cv_dw_w, cv_dw_b, cv_ln_g, cv_ln_b, cv_w_out, cv_b_out):
    bsz, seq, d = x.shape
    depth = ffn_norm.shape[0]
    nh = fox_b_f.shape[-1]
    hd = fox_q_norm.shape[-1]
    assert nh * hd == d and nh <= LANES

    ffn_w_in_b = ffn_w_in.astype(BF16)
    ffn_w_out_b = ffn_w_out.astype(BF16)
    rg_w_in_b, rg_w_a_b, rg_w_x_b, rg_w_out_b = (w.astype(BF16) for w in (rg_w_in, rg_w_a, rg_w_x, rg_w_out))
    fox_w_qkv_b = fox_w_in[:, :, :3 * d].astype(BF16)
    fox_w_f_b = jnp.pad(fox_w_in[:, :, 3 * d:], ((0, 0), (0, 0), (0, LANES - nh))).astype(BF16)
    fox_w_out_b = fox_w_out.astype(BF16)
    cv_w_in_b, cv_w_out_b = cv_w_in.astype(BF16), cv_w_out.astype(BF16)

    def row(p):
        return p[None, :]

    def pad_lanes(p):
        return jnp.pad(p, (0, LANES - p.shape[0]))[None, :]

    def twice(p):
        return jnp.concatenate([p, p])[None, :]

    for i in range(depth):
        kind, j = i % N_MIXERS, i // N_MIXERS
        x = _ffn(x.reshape(bsz * seq, d), row(ffn_norm[i, 0]), ffn_w_in_b, ffn_w_out_b, (i, 0))
        x = x.reshape(bsz, seq, d)
        g = row(mix_norm[i])
        if kind == 0:
            x = _rglru(x, g, rg_w_in_b, rg_conv_w[j], row(rg_conv_b[j]), rg_w_a_b, row(rg_b_a[j]), rg_w_x_b,
                       row(rg_b_x[j]), row(rg_lambda[j]), rg_w_out_b, (j,))
        elif kind == 1:
            q, kt, v, cum = _fox_proj(x, g, fox_w_qkv_b, fox_w_f_b, pad_lanes(fox_b_f[j]), twice(fox_q_norm[j]),
                                     twice(fox_k_norm[j]), (j,), nh)
            o = _fox_attn(q, kt, v, cum.reshape(bsz, nh // 2, 2, seq), nh)
            x = _linres(x.reshape(bsz * seq, d), o.reshape(bsz * seq, d), fox_w_out_b, (j,))
            x = x.reshape(bsz, seq, d)
        else:
            x = _conv_module(x, g, cv_w_in_b, row(cv_b_in[j]), cv_dw_w[j], row(cv_dw_b[j]), row(cv_ln_g[j]),
                             row(cv_ln_b[j]), cv_w_out_b, row(cv_b_out[j]), (j,))
        x = _ffn(x.reshape(bsz * seq, d), row(ffn_norm[i, 1]), ffn_w_in_b, ffn_w_out_b, (i, 1))
        x = x.reshape(bsz, seq, d)
    return x
```

```python
import functools

import jax
import jax.numpy as jnp
from jax import lax
from jax.experimental import pallas as pl
from jax.experimental.pallas import tpu as pltpu

F32 = jnp.float32
BF16 = jnp.bfloat16

EPS = 1e-6
N_MIXERS = 3
RG_C = 8.0

LANES = 128
SUBLANES = 8
VMEM_LIMIT_BYTES = 56 * 1024 * 1024

NEG_BIG = -1e30
LOG2E = 1.4426950408889634


def _dot(a, b):
    return jnp.dot(a, b, preferred_element_type=F32)


def _sigmoid(x):
    return 1.0 / (1.0 + jnp.exp(-x))


def _rms(x, g):
    ms = jnp.mean(x * x, axis=-1, keepdims=True)
    return x * lax.rsqrt(ms + EPS) * g


def _softplus(z):
    return jnp.maximum(z, 0.0) + jnp.log1p(jnp.exp(-jnp.abs(z)))


def _params(*semantics):
    return pltpu.CompilerParams(dimension_semantics=semantics, vmem_limit_bytes=VMEM_LIMIT_BYTES)


def _const_spec(shape):
    n = len(shape)
    return pl.BlockSpec(shape, lambda *_: (0,) * n, pipeline_mode=pl.Buffered(1))


def _layer_spec(shape, lead):
    n = len(shape)
    block = (None,) * len(lead) + tuple(shape)
    return pl.BlockSpec(block, lambda *_: tuple(lead) + (0,) * n, pipeline_mode=pl.Buffered(1))


FFN_TM = 1024
FFN_TF = 256


def _swiglu_step(x, g_ref, win_ref, wout_ref, out_ref, d_ff, tf):
    h = _rms(x, g_ref[...]).astype(BF16)
    acc = None
    for c in range(d_ff // tf):
        gate = _dot(h, win_ref[:, c * tf:(c + 1) * tf])
        up = _dot(h, win_ref[:, d_ff + c * tf:d_ff + (c + 1) * tf])
        a = (gate * _sigmoid(gate) * up).astype(BF16)
        part = _dot(a, wout_ref[c * tf:(c + 1) * tf, :])
        acc = part if acc is None else acc + part
    out_ref[...] = x + 0.5 * acc


def _ffn_kernel(x_ref, g_ref, win_ref, wout_ref, out_ref, *, d_ff, tf):
    _swiglu_step(x_ref[...], g_ref, win_ref, wout_ref, out_ref, d_ff, tf)


def _proj_ffn_kernel(x_ref, o_ref, wo_ref, g_ref, win_ref, wout_ref, out_ref, *, d_ff, tf):
    x = x_ref[...] + _dot(o_ref[...], wo_ref[...])
    _swiglu_step(x, g_ref, win_ref, wout_ref, out_ref, d_ff, tf)


def _ffn(x2, g, w_in, w_out, lead, proj=None):
    t, d = x2.shape
    d_ff = w_out.shape[-2]
    assert t % FFN_TM == 0 and d_ff % FFN_TF == 0
    tile = pl.BlockSpec((FFN_TM, d), lambda i: (i, 0))
    ffn_specs = [_const_spec((1, d)), _layer_spec((d, 2 * d_ff), lead), _layer_spec((d_ff, d), lead)]
    if proj is None:
        body, in_specs, args = _ffn_kernel, [tile] + ffn_specs, (x2, g, w_in, w_out)
    else:
        o2, wo, wo_lead = proj
        body = _proj_ffn_kernel
        in_specs = [tile, tile, _layer_spec((d, d), wo_lead)] + ffn_specs
        args = (x2, o2, wo, g, w_in, w_out)
    return pl.pallas_call(
        functools.partial(body, d_ff=d_ff, tf=FFN_TF),
        grid=(t // FFN_TM,),
        in_specs=in_specs,
        out_specs=tile,
        out_shape=jax.ShapeDtypeStruct((t, d), F32),
        compiler_params=_params("parallel"),
        name="ffn" if proj is None else "projffn",
    )(*args)


RG_TS = 512


def _rg_kernel(x_ref, g_ref, win_ref, cw_ref, cb_ref, wa_ref, ba_ref, wx_ref, bx_ref, lam_ref, wout_ref,
               o_ref, ext_scr, a_scr, b_scr, hcar_scr, *, ts, d, conv_w, nblk, blk):
    s = pl.program_id(1)
    pad = SUBLANES

    @pl.when(s == 0)
    def _():
        ext_scr[0:pad, :] = jnp.zeros((pad, d), F32)
        hcar_scr[...] = jnp.zeros_like(hcar_scr)

    x = x_ref[...]
    h = _rms(x, g_ref[...]).astype(BF16)
    gate = _dot(h, win_ref[:, 0:d])
    u = _dot(h, win_ref[:, d:2 * d])

    ext_scr[pad:pad + ts, :] = u
    ext = ext_scr[...]
    uc = u * cw_ref[conv_w - 1:conv_w, :] + cb_ref[...]
    for dly in range(1, conv_w):
        shifted = pltpu.roll(ext, dly, axis=0)[pad:pad + ts, :]
        uc = uc + shifted * cw_ref[conv_w - 1 - dly:conv_w - dly, :]
    ext_scr[0:pad, :] = u[ts - pad:ts, :]

    ub = uc.astype(BF16)
    ra = jnp.concatenate([_dot(ub[:, n * blk:(n + 1) * blk], wa_ref[n]) for n in range(nblk)], axis=-1)
    rx = jnp.concatenate([_dot(ub[:, n * blk:(n + 1) * blk], wx_ref[n]) for n in range(nblk)], axis=-1)
    r = _sigmoid(ra + ba_ref[...])
    ig = _sigmoid(rx + bx_ref[...])
    log_a = (-RG_C) * r * _softplus(-lam_ref[...])
    a = jnp.exp(log_a)
    a_scr[...] = a
    b_scr[...] = jnp.sqrt(-jnp.tanh(log_a) * (a * a + 1.0)) * (ig * uc)

    row = lax.broadcasted_iota(jnp.int32, (SUBLANES, d), 0)

    def group(gi, carry):
        r0 = pl.multiple_of(gi * SUBLANES, SUBLANES)
        a = a_scr[pl.ds(r0, SUBLANES), :]
        b = b_scr[pl.ds(r0, SUBLANES), :]
        for sh in (1, 2, 4):
            keep = row >= sh
            a_s = jnp.where(keep, pltpu.roll(a, sh, axis=0), 1.0)
            b_s = jnp.where(keep, pltpu.roll(b, sh, axis=0), 0.0)
            b = a * b_s + b
            a = a * a_s
        hg = a * carry + b
        b_scr[pl.ds(r0, SUBLANES), :] = hg
        return hg[SUBLANES - 1:SUBLANES, :]

    hcar_scr[...] = lax.fori_loop(0, ts // SUBLANES, group, hcar_scr[...], unroll=4)

    y = (b_scr[...] * jax.nn.gelu(gate)).astype(BF16)
    o_ref[...] = x + _dot(y, wout_ref[...])


def _rglru(x, g, w_in, conv_w, conv_b, w_a, b_a, w_x, b_x, lam, w_out, lead):
    bsz, seq, d = x.shape
    kw = conv_w.shape[-2]
    nblk, blk = w_a.shape[-3], w_a.shape[-1]
    ts = RG_TS
    assert seq % ts == 0 and kw - 1 <= SUBLANES and nblk * blk == d
    return pl.pallas_call(
        functools.partial(_rg_kernel, ts=ts, d=d, conv_w=kw, nblk=nblk, blk=blk),
        grid=(bsz, seq // ts),
        in_specs=[
            pl.BlockSpec((None, ts, d), lambda b, s: (b, s, 0)),
            _const_spec((1, d)),
            _layer_spec((d, 2 * d), lead),
            _const_spec((kw, d)),
            _const_spec((1, d)),
            _layer_spec((nblk, blk, blk), lead),
            _const_spec((1, d)),
            _layer_spec((nblk, blk, blk), lead),
            _const_spec((1, d)),
            _const_spec((1, d)),
            _layer_spec((d, d), lead),
        ],
        out_specs=pl.BlockSpec((None, ts, d), lambda b, s: (b, s, 0)),
        out_shape=jax.ShapeDtypeStruct((bsz, seq, d), F32),
        scratch_shapes=[
            pltpu.VMEM((ts + SUBLANES, d), F32),
            pltpu.VMEM((ts, d), F32),
            pltpu.VMEM((ts, d), F32),
            pltpu.VMEM((1, d), F32),
        ],
        compiler_params=_params("parallel", "arbitrary"),
        name="rglru",
    )(x, g, w_in, conv_w, conv_b, w_a, b_a, w_x, b_x, lam, w_out)


CV_TS = 256
CV_ROWS = 128
CV_HIST = 32


def _cv_kernel(x_ref, g_ref, win_ref, bin_ref, dw_ref, dwb_ref, lng_ref, lnb_ref, wout_ref, bout_ref,
               o_ref, ext_scr, y_scr, *, ts, d, kw):
    s = pl.program_id(1)

    @pl.when(s == 0)
    def _():
        ext_scr[0:CV_HIST, :] = jnp.zeros((CV_HIST, d), F32)

    x = x_ref[...]
    h = _rms(x, g_ref[...]).astype(BF16)
    val = _dot(h, win_ref[:, 0:d]) + bin_ref[:, 0:d]
    gate = _dot(h, win_ref[:, d:2 * d]) + bin_ref[:, d:2 * d]
    hh = val * _sigmoid(gate)
    ext_scr[CV_HIST:CV_HIST + ts, :] = hh

    def col_block(c, carry):
        c0 = pl.multiple_of(c * LANES, LANES)
        w = dw_ref[:, pl.ds(c0, LANES)]
        bias = dwb_ref[:, pl.ds(c0, LANES)]
        for r0 in range(0, ts, CV_ROWS):
            slab = ext_scr[r0:r0 + CV_ROWS + CV_HIST, pl.ds(c0, LANES)]
            acc = jnp.zeros((CV_ROWS, LANES), F32) + bias
            for r in range(SUBLANES):
                rolled = slab if r == 0 else pltpu.roll(slab, r, axis=0)
                for a in range(CV_HIST // SUBLANES):
                    dly = SUBLANES * a + r
                    if dly < kw:
                        lo = CV_HIST - SUBLANES * a
                        acc = acc + rolled[lo:lo + CV_ROWS, :] * w[kw - 1 - dly:kw - dly, :]
            y_scr[r0:r0 + CV_ROWS, pl.ds(c0, LANES)] = acc
        return carry

    lax.fori_loop(0, d // LANES, col_block, 0)
    ext_scr[0:CV_HIST, :] = hh[ts - CV_HIST:ts, :]

    y = y_scr[...]
    mu = jnp.mean(y, axis=-1, keepdims=True)
    yc = y - mu
    var = jnp.mean(yc * yc, axis=-1, keepdims=True)
    z = yc * lax.rsqrt(var + EPS) * lng_ref[...] + lnb_ref[...]
    z = (z * _sigmoid(z)).astype(BF16)
    o_ref[...] = x + _dot(z, wout_ref[...]) + bout_ref[...]


def _conv_module(x, g, w_in, b_in, dw_w, dw_b, ln_g, ln_b, w_out, b_out, lead):
    bsz, seq, d = x.shape
    kw = dw_w.shape[-2]
    ts = CV_TS
    assert seq % ts == 0 and kw - 1 <= CV_HIST and ts % CV_ROWS == 0
    return pl.pallas_call(
        functools.partial(_cv_kernel, ts=ts, d=d, kw=kw),
        grid=(bsz, seq // ts),
        in_specs=[
            pl.BlockSpec((None, ts, d), lambda b, s: (b, s, 0)),
            _const_spec((1, d)),
            _layer_spec((d, 2 * d), lead),
            _const_spec((1, 2 * d)),
            _const_spec((kw, d)),
            _const_spec((1, d)),
            _const_spec((1, d)),
            _const_spec((1, d)),
            _layer_spec((d, d), lead),
            _const_spec((1, d)),
        ],
        out_specs=pl.BlockSpec((None, ts, d), lambda b, s: (b, s, 0)),
        out_shape=jax.ShapeDtypeStruct((bsz, seq, d), F32),
        scratch_shapes=[
            pltpu.VMEM((ts + CV_HIST, d), F32),
            pltpu.VMEM((ts, d), F32),
        ],
        compiler_params=_params("parallel", "arbitrary"),
        name="convmod",
    )(x, g, w_in, b_in, dw_w, dw_b, ln_g, ln_b, w_out, b_out)


FX_TS = 512
FX_TQ = 512


def _head_rms(t, g2, lane_lo, hd):
    sq = t * t
    s_lo = jnp.sum(jnp.where(lane_lo, sq, 0.0), axis=-1, keepdims=True)
    s_hi = jnp.sum(jnp.where(lane_lo, 0.0, sq), axis=-1, keepdims=True)
    ms = jnp.where(lane_lo, s_lo, s_hi) * (1.0 / hd)
    return t * lax.rsqrt(ms + EPS) * g2


def _fxproj_kernel(x_ref, g_ref, wqkv_ref, wf_ref, bf_ref, qg_ref, kg_ref,
                   q_ref, kt_ref, v_ref, cum_ref, car_scr, *, ts, d, hd, nh):
    s = pl.program_id(1)

    @pl.when(s == 0)
    def _():
        car_scr[...] = jnp.zeros_like(car_scr)

    x = x_ref[...]
    h = _rms(x, g_ref[...]).astype(BF16)
    lane_lo = lax.broadcasted_iota(jnp.int32, (ts, LANES), 1) < hd
    scale = hd ** -0.5 * LOG2E
    q = _dot(h, wqkv_ref[:, 0:d])
    k = _dot(h, wqkv_ref[:, d:2 * d])
    for c in range(d // LANES):
        cols = slice(c * LANES, (c + 1) * LANES)
        q_ref[:, cols] = (_head_rms(q[:, cols], qg_ref[...], lane_lo, hd) * scale).astype(BF16)
        kt_ref[cols, :] = jnp.transpose(_head_rms(k[:, cols], kg_ref[...], lane_lo, hd)).astype(BF16)
    v_ref[...] = _dot(h, wqkv_ref[:, 2 * d:3 * d]).astype(BF16)

    f = _dot(h, wf_ref[...]) + bf_ref[...]
    cum = jnp.minimum(f, 0.0) - jnp.log1p(jnp.exp(-jnp.abs(f)))
    row = lax.broadcasted_iota(jnp.int32, (ts, LANES), 0)
    sh = 1
    while sh < ts:
        cum = cum + jnp.where(row >= sh, pltpu.roll(cum, sh, axis=0), 0.0)
        sh *= 2
    cum = cum + car_scr[...]
    car_scr[...] = cum[ts - 1:ts, :]
    cum_ref[...] = jnp.transpose(cum * LOG2E)[0:nh, :]


def _fox_proj(x, g, w_qkv, w_f, b_f, q_g2, k_g2, lead, nh):
    bsz, seq, d = x.shape
    hd = d // nh
    ts = FX_TS
    assert seq % ts == 0 and 2 * hd == LANES and nh % SUBLANES == 0
    tile = pl.BlockSpec((None, ts, d), lambda b, s: (b, s, 0))
    return pl.pallas_call(
        functools.partial(_fxproj_kernel, ts=ts, d=d, hd=hd, nh=nh),
        grid=(bsz, seq // ts),
        in_specs=[
            tile,
            _const_spec((1, d)),
            _layer_spec((d, 3 * d), lead),
            _layer_spec((d, LANES), lead),
            _const_spec((1, LANES)),
            _const_spec((1, LANES)),
            _const_spec((1, LANES)),
        ],
        out_specs=[tile, pl.BlockSpec((None, d, ts), lambda b, s: (b, 0, s)), tile,
                   pl.BlockSpec((None, nh, ts), lambda b, s: (b, 0, s))],
        out_shape=[
            jax.ShapeDtypeStruct((bsz, seq, d), BF16),
            jax.ShapeDtypeStruct((bsz, d, seq), BF16),
            jax.ShapeDtypeStruct((bsz, seq, d), BF16),
            jax.ShapeDtypeStruct((bsz, nh, seq), F32),
        ],
        scratch_shapes=[pltpu.VMEM((1, LANES), F32)],
        compiler_params=_params("parallel", "arbitrary"),
        name="foxproj",
    )(x, g, w_qkv, w_f, b_f, q_g2, k_g2)


def _fxattn_kernel(q_ref, kt_ref, v_ref, cum_ref, o_ref, m0_scr, m1_scr, acc0_scr, acc1_scr, *, tq, hd):
    i = pl.program_id(2)
    q = q_ref[...]
    lane_lo = lax.broadcasted_iota(jnp.int32, (tq, LANES), 1) < hd
    zero = jnp.zeros_like(q)
    qs = (jnp.where(lane_lo, q, zero), jnp.where(lane_lo, zero, q))
    m_scrs = (m0_scr, m1_scr)
    acc_scrs = (acc0_scr, acc1_scr)

    def scores(hh, j0, tk):
        return _dot(qs[hh], kt_ref[:, pl.ds(j0, tk)]) - cum_ref[hh:hh + 1, pl.ds(j0, tk)]

    def values(j0, tk):
        vb = v_ref[pl.ds(j0, tk), :]
        lo = lax.broadcasted_iota(jnp.int32, (tk, LANES), 1) < hd
        one = jnp.ones_like(vb)
        return (jnp.where(lo, vb, one), jnp.where(lo, one, vb))

    def row_max(sc):
        return jnp.broadcast_to(jnp.max(sc, axis=-1, keepdims=True), (tq, LANES))

    def lanes_to(m, tk):
        return jnp.concatenate([m] * (tk // LANES), axis=1)

    d0 = pl.multiple_of(i * tq, tq)
    half = tq // 2
    for r0, tk in ((0, half), (half, tq)):
        vs = values(d0, tk)
        causal = (lax.broadcasted_iota(jnp.int32, (half, tk), 0) + r0
                  >= lax.broadcasted_iota(jnp.int32, (half, tk), 1))
        for hh in range(2):
            sc = _dot(qs[hh][r0:r0 + half, :], kt_ref[:, pl.ds(d0, tk)]) - cum_ref[hh:hh + 1, pl.ds(d0, tk)]
            sc = jnp.where(causal, sc, NEG_BIG)
            m = jnp.broadcast_to(jnp.max(sc, axis=-1, keepdims=True), (half, LANES))
            m_scrs[hh][r0:r0 + half, :] = m
            acc_scrs[hh][r0:r0 + half, :] = _dot(jnp.exp2(sc - lanes_to(m, tk)).astype(BF16), vs[hh])

    def kv_block(j0, tk):
        vs = values(j0, tk)
        for hh in range(2):
            sc = scores(hh, j0, tk)
            m_old = m_scrs[hh][...]
            m_new = jnp.maximum(m_old, row_max(sc))
            m_scrs[hh][...] = m_new
            p = jnp.exp2(sc - lanes_to(m_new, tk)).astype(BF16)
            acc_scrs[hh][...] = jnp.exp2(m_old - m_new) * acc_scrs[hh][...] + _dot(p, vs[hh])

    def wide_block(j, carry):
        kv_block(pl.multiple_of(j * 2 * tq, 2 * tq), 2 * tq)
        return carry

    lax.fori_loop(0, i // 2, wide_block, 0)

    @pl.when(i % 2 == 1)
    def _():
        kv_block(pl.multiple_of((i - 1) * tq, tq), tq)

    acc_lo, acc_hi = acc0_scr[...], acc1_scr[...]
    o_lo = acc_lo / acc_lo[:, LANES - 1:LANES]
    o_hi = acc_hi / acc_hi[:, 0:1]
    o_ref[...] = jnp.where(lane_lo, o_lo, o_hi).astype(BF16)


def _fox_attn(q, kt, v, cum4, nh):
    bsz, seq, d = q.shape
    hd = d // nh
    tq = FX_TQ
    assert seq % tq == 0 and 2 * hd == LANES
    return pl.pallas_call(
        functools.partial(_fxattn_kernel, tq=tq, hd=hd),
        grid=(bsz, nh // 2, seq // tq),
        in_specs=[
            pl.BlockSpec((None, tq, LANES), lambda b, p, i: (b, i, p)),
            pl.BlockSpec((None, LANES, seq), lambda b, p, i: (b, p, 0)),
            pl.BlockSpec((None, seq, LANES), lambda b, p, i: (b, 0, p)),
            pl.BlockSpec((None, None, 2, seq), lambda b, p, i: (b, p, 0, 0)),
        ],
        out_specs=pl.BlockSpec((None, tq, LANES), lambda b, p, i: (b, i, p)),
        out_shape=jax.ShapeDtypeStruct((bsz, seq, d), BF16),
        scratch_shapes=[pltpu.VMEM((tq, LANES), F32)] * 4,
        compiler_params=_params("parallel", "parallel", "arbitrary"),
        name="foxattn",
    )(q, kt, v, cum4)


def kernel(x, ffn_norm, ffn_w_in, ffn_w_out, mix_norm, rg_w_in, rg_conv_w, rg_conv_b, rg_w_a, rg_b_a, rg_w_x,
           rg_b_x, rg_lambda, rg_w_out, fox_w_in, fox_b_f, fox_q_norm, fox_k_norm, fox_w_out, cv_w_in, cv_b_in,
           cv_dw_w, cv_dw_b, cv_ln_g, cv_ln_b, cv_w_out, cv_b_out):
    bsz, seq, d = x.shape
    depth = ffn_norm.shape[0]
    nh = fox_b_f.shape[-1]
    hd = fox_q_norm.shape[-1]
    assert nh * hd == d and nh <= LANES

    ffn_w_in_b = ffn_w_in.astype(BF16)
    ffn_w_out_b = ffn_w_out.astype(BF16)
    rg_w_in_b, rg_w_a_b, rg_w_x_b, rg_w_out_b = (w.astype(BF16) for w in (rg_w_in, rg_w_a, rg_w_x, rg_w_out))
    fox_w_qkv_b = fox_w_in[:, :, :3 * d].astype(BF16)
    fox_w_f_b = jnp.pad(fox_w_in[:, :, 3 * d:], ((0, 0), (0, 0), (0, LANES - nh))).astype(BF16)
    fox_w_out_b = fox_w_out.astype(BF16)
    cv_w_in_b, cv_w_out_b = cv_w_in.astype(BF16), cv_w_out.astype(BF16)

    def row(p):
        return p[None, :]

    def pad_lanes(p):
        return jnp.pad(p, (0, LANES - p.shape[0]))[None, :]

    def twice(p):
        return jnp.concatenate([p, p])[None, :]

    for i in range(depth):
        kind, j = i % N_MIXERS, i // N_MIXERS
        x = _ffn(x.reshape(bsz * seq, d), row(ffn_norm[i, 0]), ffn_w_in_b, ffn_w_out_b, (i, 0))
        x = x.reshape(bsz, seq, d)
        g = row(mix_norm[i])
        proj = None
        if kind == 0:
            x = _rglru(x, g, rg_w_in_b, rg_conv_w[j], row(rg_conv_b[j]), rg_w_a_b, row(rg_b_a[j]), rg_w_x_b,
                       row(rg_b_x[j]), row(rg_lambda[j]), rg_w_out_b, (j,))
        elif kind == 1:
            q, kt, v, cum = _fox_proj(x, g, fox_w_qkv_b, fox_w_f_b, pad_lanes(fox_b_f[j]), twice(fox_q_norm[j]),
                                     twice(fox_k_norm[j]), (j,), nh)
            o = _fox_attn(q, kt, v, cum.reshape(bsz, nh // 2, 2, seq), nh)
            proj = (o.reshape(bsz * seq, d), fox_w_out_b, (j,))
        else:
            x = _conv_module(x, g, cv_w_in_b, row(cv_b_in[j]), cv_dw_w[j], row(cv_dw_b[j]), row(cv_ln_g[j]),
                             row(cv_ln_b[j]), cv_w_out_b, row(cv_b_out[j]), (j,))
        x = _ffn(x.reshape(bsz * seq, d), row(ffn_norm[i, 1]), ffn_w_in_b, ffn_w_out_b, (i, 1), proj)
        x = x.reshape(bsz, seq, d)
    return x
```

```python
import functools

import jax
import jax.numpy as jnp
from jax import lax
from jax.experimental import pallas as pl
from jax.experimental.pallas import tpu as pltpu

F32 = jnp.float32
BF16 = jnp.bfloat16

EPS = 1e-6
N_MIXERS = 3
RG_C = 8.0

LANES = 128
SUBLANES = 8
VMEM_LIMIT_BYTES = 56 * 1024 * 1024

NEG_BIG = -1e30
LOG2E = 1.4426950408889634


def _dot(a, b):
    return jnp.dot(a, b, preferred_element_type=F32)


def _sigmoid(x):
    return 1.0 / (1.0 + jnp.exp(-x))


def _rms(x, g):
    ms = jnp.mean(x * x, axis=-1, keepdims=True)
    return x * lax.rsqrt(ms + EPS) * g


def _softplus(z):
    return jnp.maximum(z, 0.0) + jnp.log1p(jnp.exp(-jnp.abs(z)))


def _params(*semantics):
    return pltpu.CompilerParams(dimension_semantics=semantics, vmem_limit_bytes=VMEM_LIMIT_BYTES)


def _const_spec(shape):
    n = len(shape)
    return pl.BlockSpec(shape, lambda *_: (0,) * n, pipeline_mode=pl.Buffered(1))


def _layer_spec(shape, lead):
    n = len(shape)
    block = (None,) * len(lead) + tuple(shape)
    return pl.BlockSpec(block, lambda *_: tuple(lead) + (0,) * n, pipeline_mode=pl.Buffered(1))


FFN_TM = 1024
FFN_TF = 256


def _swiglu_step(x, g_ref, win_ref, wout_ref, out_ref, d_ff, tf):
    h = _rms(x, g_ref[...]).astype(BF16)
    acc = None
    for c in range(d_ff // tf):
        gate = _dot(h, win_ref[:, c * tf:(c + 1) * tf])
        up = _dot(h, win_ref[:, d_ff + c * tf:d_ff + (c + 1) * tf])
        a = (gate * _sigmoid(gate) * up).astype(BF16)
        part = _dot(a, wout_ref[c * tf:(c + 1) * tf, :])
        acc = part if acc is None else acc + part
    out_ref[...] = x + 0.5 * acc


def _ffn_kernel(x_ref, g_ref, win_ref, wout_ref, out_ref, *, d_ff, tf):
    _swiglu_step(x_ref[...], g_ref, win_ref, wout_ref, out_ref, d_ff, tf)


def _proj_ffn_kernel(x_ref, o_ref, wo_ref, g_ref, win_ref, wout_ref, out_ref, *, d_ff, tf):
    x = x_ref[...] + _dot(o_ref[...], wo_ref[...])
    _swiglu_step(x, g_ref, win_ref, wout_ref, out_ref, d_ff, tf)


def _ffn(x2, g, w_in, w_out, lead, proj=None):
    t, d = x2.shape
    d_ff = w_out.shape[-2]
    assert t % FFN_TM == 0 and d_ff % FFN_TF == 0
    tile = pl.BlockSpec((FFN_TM, d), lambda i: (i, 0))
    ffn_specs = [_const_spec((1, d)), _layer_spec((d, 2 * d_ff), lead), _layer_spec((d_ff, d), lead)]
    if proj is None:
        body, in_specs, args = _ffn_kernel, [tile] + ffn_specs, (x2, g, w_in, w_out)
    else:
        o2, wo, wo_lead = proj
        body = _proj_ffn_kernel
        in_specs = [tile, tile, _layer_spec((d, d), wo_lead)] + ffn_specs
        args = (x2, o2, wo, g, w_in, w_out)
    return pl.pallas_call(
        functools.partial(body, d_ff=d_ff, tf=FFN_TF),
        grid=(t // FFN_TM,),
        in_specs=in_specs,
        out_specs=tile,
        out_shape=jax.ShapeDtypeStruct((t, d), F32),
        compiler_params=_params("parallel"),
        name="ffn" if proj is None else "projffn",
    )(*args)


RG_TS = 512


def _rg_kernel(x_ref, g_ref, win_ref, cw_ref, cb_ref, wa_ref, ba_ref, wx_ref, bx_ref, lam_ref, wout_ref,
               o_ref, ext_scr, a_scr, b_scr, hcar_scr, *, ts, d, conv_w, nblk, blk):
    s = pl.program_id(1)
    pad = SUBLANES

    @pl.when(s == 0)
    def _():
        ext_scr[0:pad, :] = jnp.zeros((pad, d), F32)
        hcar_scr[...] = jnp.zeros_like(hcar_scr)

    x = x_ref[...]
    h = _rms(x, g_ref[...]).astype(BF16)
    gate = _dot(h, win_ref[:, 0:d])
    u = _dot(h, win_ref[:, d:2 * d])

    ext_scr[pad:pad + ts, :] = u
    ext = ext_scr[...]
    uc = u * cw_ref[conv_w - 1:conv_w, :] + cb_ref[...]
    for dly in range(1, conv_w):
        shifted = pltpu.roll(ext, dly, axis=0)[pad:pad + ts, :]
        uc = uc + shifted * cw_ref[conv_w - 1 - dly:conv_w - dly, :]
    ext_scr[0:pad, :] = u[ts - pad:ts, :]

    ub = uc.astype(BF16)
    ra = jnp.concatenate([_dot(ub[:, n * blk:(n + 1) * blk], wa_ref[n]) for n in range(nblk)], axis=-1)
    rx = jnp.concatenate([_dot(ub[:, n * blk:(n + 1) * blk], wx_ref[n]) for n in range(nblk)], axis=-1)
    r = _sigmoid(ra + ba_ref[...])
    ig = _sigmoid(rx + bx_ref[...])
    log_a = (-RG_C) * r * _softplus(-lam_ref[...])
    a = jnp.exp(log_a)
    a_scr[...] = a
    b_scr[...] = jnp.sqrt(-jnp.tanh(log_a) * (a * a + 1.0)) * (ig * uc)

    row = lax.broadcasted_iota(jnp.int32, (SUBLANES, d), 0)

    def group(gi, carry):
        r0 = pl.multiple_of(gi * SUBLANES, SUBLANES)
        a = a_scr[pl.ds(r0, SUBLANES), :]
        b = b_scr[pl.ds(r0, SUBLANES), :]
        for sh in (1, 2, 4):
            keep = row >= sh
            a_s = jnp.where(keep, pltpu.roll(a, sh, axis=0), 1.0)
            b_s = jnp.where(keep, pltpu.roll(b, sh, axis=0), 0.0)
            b = a * b_s + b
            a = a * a_s
        hg = a * carry + b
        b_scr[pl.ds(r0, SUBLANES), :] = hg
        return hg[SUBLANES - 1:SUBLANES, :]

    hcar_scr[...] = lax.fori_loop(0, ts // SUBLANES, group, hcar_scr[...], unroll=4)

    y = (b_scr[...] * jax.nn.gelu(gate)).astype(BF16)
    o_ref[...] = x + _dot(y, wout_ref[...])


def _rglru(x, g, w_in, conv_w, conv_b, w_a, b_a, w_x, b_x, lam, w_out, lead):
    bsz, seq, d = x.shape
    kw = conv_w.shape[-2]
    nblk, blk = w_a.shape[-3], w_a.shape[-1]
    ts = RG_TS
    assert seq % ts == 0 and kw - 1 <= SUBLANES and nblk * blk == d
    return pl.pallas_call(
        functools.partial(_rg_kernel, ts=ts, d=d, conv_w=kw, nblk=nblk, blk=blk),
        grid=(bsz, seq // ts),
        in_specs=[
            pl.BlockSpec((None, ts, d), lambda b, s: (b, s, 0)),
            _const_spec((1, d)),
            _layer_spec((d, 2 * d), lead),
            _const_spec((kw, d)),
            _const_spec((1, d)),
            _layer_spec((nblk, blk, blk), lead),
            _const_spec((1, d)),
            _layer_spec((nblk, blk, blk), lead),
            _const_spec((1, d)),
            _const_spec((1, d)),
            _layer_spec((d, d), lead),
        ],
        out_specs=pl.BlockSpec((None, ts, d), lambda b, s: (b, s, 0)),
        out_shape=jax.ShapeDtypeStruct((bsz, seq, d), F32),
        scratch_shapes=[
            pltpu.VMEM((ts + SUBLANES, d), F32),
            pltpu.VMEM((ts, d), F32),
            pltpu.VMEM((ts, d), F32),
            pltpu.VMEM((1, d), F32),
        ],
        compiler_params=_params("parallel", "arbitrary"),
        name="rglru",
    )(x, g, w_in, conv_w, conv_b, w_a, b_a, w_x, b_x, lam, w_out)


CV_TS = 256
CV_ROWS = 128
CV_HIST = 32


def _cv_kernel(x_ref, g_ref, win_ref, bin_ref, dw_ref, dwb_ref, lng_ref, lnb_ref, wout_ref, bout_ref,
               o_ref, ext_scr, y_scr, *, ts, d, kw):
    s = pl.program_id(1)

    @pl.when(s == 0)
    def _():
        ext_scr[0:CV_HIST, :] = jnp.zeros((CV_HIST, d), F32)

    x = x_ref[...]
    h = _rms(x, g_ref[...]).astype(BF16)
    val = _dot(h, win_ref[:, 0:d]) + bin_ref[:, 0:d]
    gate = _dot(h, win_ref[:, d:2 * d]) + bin_ref[:, d:2 * d]
    hh = val * _sigmoid(gate)
    ext_scr[CV_HIST:CV_HIST + ts, :] = hh

    def col_block(c, carry):
        c0 = pl.multiple_of(c * LANES, LANES)
        w = dw_ref[:, pl.ds(c0, LANES)]
        bias = dwb_ref[:, pl.ds(c0, LANES)]
        for r0 in range(0, ts, CV_ROWS):
            slab = ext_scr[r0:r0 + CV_ROWS + CV_HIST, pl.ds(c0, LANES)]
            acc = jnp.zeros((CV_ROWS, LANES), F32) + bias
            for r in range(SUBLANES):
                rolled = slab if r == 0 else pltpu.roll(slab, r, axis=0)
                for a in range(CV_HIST // SUBLANES):
                    dly = SUBLANES * a + r
                    if dly < kw:
                        lo = CV_HIST - SUBLANES * a
                        acc = acc + rolled[lo:lo + CV_ROWS, :] * w[kw - 1 - dly:kw - dly, :]
            y_scr[r0:r0 + CV_ROWS, pl.ds(c0, LANES)] = acc
        return carry

    lax.fori_loop(0, d // LANES, col_block, 0)
    ext_scr[0:CV_HIST, :] = hh[ts - CV_HIST:ts, :]

    y = y_scr[...]
    mu = jnp.mean(y, axis=-1, keepdims=True)
    yc = y - mu
    var = jnp.mean(yc * yc, axis=-1, keepdims=True)
    z = yc * lax.rsqrt(var + EPS) * lng_ref[...] + lnb_ref[...]
    z = (z * _sigmoid(z)).astype(BF16)
    o_ref[...] = x + _dot(z, wout_ref[...]) + bout_ref[...]


def _conv_module(x, g, w_in, b_in, dw_w, dw_b, ln_g, ln_b, w_out, b_out, lead):
    bsz, seq, d = x.shape
    kw = dw_w.shape[-2]
    ts = CV_TS
    assert seq % ts == 0 and kw - 1 <= CV_HIST and ts % CV_ROWS == 0
    return pl.pallas_call(
        functools.partial(_cv_kernel, ts=ts, d=d, kw=kw),
        grid=(bsz, seq // ts),
        in_specs=[
            pl.BlockSpec((None, ts, d), lambda b, s: (b, s, 0)),
            _const_spec((1, d)),
            _layer_spec((d, 2 * d), lead),
            _const_spec((1, 2 * d)),
            _const_spec((kw, d)),
            _const_spec((1, d)),
            _const_spec((1, d)),
            _const_spec((1, d)),
            _layer_spec((d, d), lead),
            _const_spec((1, d)),
        ],
        out_specs=pl.BlockSpec((None, ts, d), lambda b, s: (b, s, 0)),
        out_shape=jax.ShapeDtypeStruct((bsz, seq, d), F32),
        scratch_shapes=[
            pltpu.VMEM((ts + CV_HIST, d), F32),
            pltpu.VMEM((ts, d), F32),
        ],
        compiler_params=_params("parallel", "arbitrary"),
        name="convmod",
    )(x, g, w_in, b_in, dw_w, dw_b, ln_g, ln_b, w_out, b_out)


FX_TS = 512
FX_TQ = 512


def _head_rms(t, g2, lane_lo, hd):
    sq = t * t
    s_lo = jnp.sum(jnp.where(lane_lo, sq, 0.0), axis=-1, keepdims=True)
    s_hi = jnp.sum(jnp.where(lane_lo, 0.0, sq), axis=-1, keepdims=True)
    ms = jnp.where(lane_lo, s_lo, s_hi) * (1.0 / hd)
    return t * lax.rsqrt(ms + EPS) * g2


def _fxproj_kernel(x_ref, g_ref, wqkv_ref, wf_ref, bf_ref, qg_ref, kg_ref,
                   q_ref, kt_ref, v_ref, cum_ref, car_scr, *, ts, d, hd, nh):
    s = pl.program_id(1)

    @pl.when(s == 0)
    def _():
        car_scr[...] = jnp.zeros_like(car_scr)

    x = x_ref[...]
    h = _rms(x, g_ref[...]).astype(BF16)
    lane_lo = lax.broadcasted_iota(jnp.int32, (ts, LANES), 1) < hd
    scale = hd ** -0.5 * LOG2E
    q = _dot(h, wqkv_ref[:, 0:d])
    k = _dot(h, wqkv_ref[:, d:2 * d])
    for c in range(d // LANES):
        cols = slice(c * LANES, (c + 1) * LANES)
        q_ref[:, cols] = (_head_rms(q[:, cols], qg_ref[...], lane_lo, hd) * scale).astype(BF16)
        kt_ref[cols, :] = jnp.transpose(_head_rms(k[:, cols], kg_ref[...], lane_lo, hd)).astype(BF16)
    v_ref[...] = _dot(h, wqkv_ref[:, 2 * d:3 * d]).astype(BF16)

    f = _dot(h, wf_ref[...]) + bf_ref[...]
    cum = jnp.minimum(f, 0.0) - jnp.log1p(jnp.exp(-jnp.abs(f)))
    row = lax.broadcasted_iota(jnp.int32, (ts, LANES), 0)
    sh = 1
    while sh < ts:
        cum = cum + jnp.where(row >= sh, pltpu.roll(cum, sh, axis=0), 0.0)
        sh *= 2
    cum = cum + car_scr[...]
    car_scr[...] = cum[ts - 1:ts, :]
    cum_ref[...] = jnp.transpose(cum * LOG2E)[0:nh, :]


def _fox_proj(x, g, w_qkv, w_f, b_f, q_g2, k_g2, lead, nh):
    bsz, seq, d = x.shape
    hd = d // nh
    ts = FX_TS
    assert seq % ts == 0 and 2 * hd == LANES and nh % SUBLANES == 0
    tile = pl.BlockSpec((None, ts, d), lambda b, s: (b, s, 0))
    return pl.pallas_call(
        functools.partial(_fxproj_kernel, ts=ts, d=d, hd=hd, nh=nh),
        grid=(bsz, seq // ts),
        in_specs=[
            tile,
            _const_spec((1, d)),
            _layer_spec((d, 3 * d), lead),
            _layer_spec((d, LANES), lead),
            _const_spec((1, LANES)),
            _const_spec((1, LANES)),
            _const_spec((1, LANES)),
        ],
        out_specs=[tile, pl.BlockSpec((None, d, ts), lambda b, s: (b, 0, s)), tile,
                   pl.BlockSpec((None, nh, ts), lambda b, s: (b, 0, s))],
        out_shape=[
            jax.ShapeDtypeStruct((bsz, seq, d), BF16),
            jax.ShapeDtypeStruct((bsz, d, seq), BF16),
            jax.ShapeDtypeStruct((bsz, seq, d), BF16),
            jax.ShapeDtypeStruct((bsz, nh, seq), F32),
        ],
        scratch_shapes=[pltpu.VMEM((1, LANES), F32)],
        compiler_params=_params("parallel", "arbitrary"),
        name="foxproj",
    )(x, g, w_qkv, w_f, b_f, q_g2, k_g2)


def _fxattn_kernel(q_ref, kt_ref, v_ref, cum_ref, o_ref, m0_scr, m1_scr, acc0_scr, acc1_scr, *, tq, hd):
    i = pl.program_id(2)
    q = q_ref[...]
    lane_lo = lax.broadcasted_iota(jnp.int32, (tq, LANES), 1) < hd
    zero = jnp.zeros_like(q)
    qs = (jnp.where(lane_lo, q, zero), jnp.where(lane_lo, zero, q))
    m_scrs = (m0_scr, m1_scr)
    acc_scrs = (acc0_scr, acc1_scr)

    def scores(hh, j0, tk):
        return _dot(qs[hh], kt_ref[:, pl.ds(j0, tk)]) - cum_ref[hh:hh + 1, pl.ds(j0, tk)]

    def values(j0, tk):
        vb = v_ref[pl.ds(j0, tk), :]
        lo = lax.broadcasted_iota(jnp.int32, (tk, LANES), 1) < hd
        one = jnp.ones_like(vb)
        return (jnp.where(lo, vb, one), jnp.where(lo, one, vb))

    def row_max(sc):
        return jnp.broadcast_to(jnp.max(sc, axis=-1, keepdims=True), (tq, LANES))

    def lanes_to(m, tk):
        return jnp.concatenate([m] * (tk // LANES), axis=1)

    d0 = pl.multiple_of(i * tq, tq)
    half = tq // 2
    parts = [(r0, tk, hh) for r0, tk in ((0, half), (half, tq)) for hh in range(2)]
    vss = {tk: values(d0, tk) for tk in (half, tq)}
    scs = []
    for r0, tk, hh in parts:
        causal = (lax.broadcasted_iota(jnp.int32, (half, tk), 0) + r0
                  >= lax.broadcasted_iota(jnp.int32, (half, tk), 1))
        sc = _dot(qs[hh][r0:r0 + half, :], kt_ref[:, pl.ds(d0, tk)]) - cum_ref[hh:hh + 1, pl.ds(d0, tk)]
        scs.append(jnp.where(causal, sc, NEG_BIG))
    ms = [jnp.broadcast_to(jnp.max(sc, axis=-1, keepdims=True), (half, LANES)) for sc in scs]
    ps = [jnp.exp2(sc - lanes_to(m, tk)).astype(BF16) for sc, m, (r0, tk, hh) in zip(scs, ms, parts)]
    for p, m, (r0, tk, hh) in zip(ps, ms, parts):
        m_scrs[hh][r0:r0 + half, :] = m
        acc_scrs[hh][r0:r0 + half, :] = _dot(p, vss[tk][hh])

    def kv_block(j0, tk):
        vs = values(j0, tk)
        scs = [scores(hh, j0, tk) for hh in range(2)]
        m_olds = [m_scrs[hh][...] for hh in range(2)]
        m_news = [jnp.maximum(m_olds[hh], row_max(scs[hh])) for hh in range(2)]
        ps = [jnp.exp2(scs[hh] - lanes_to(m_news[hh], tk)).astype(BF16) for hh in range(2)]
        for hh in range(2):
            m_scrs[hh][...] = m_news[hh]
            acc_scrs[hh][...] = jnp.exp2(m_olds[hh] - m_news[hh]) * acc_scrs[hh][...] + _dot(ps[hh], vs[hh])

    def wide_block(j, carry):
        kv_block(pl.multiple_of(j * 2 * tq, 2 * tq), 2 * tq)
        return carry

    lax.fori_loop(0, i // 2, wide_block, 0)

    @pl.when(i % 2 == 1)
    def _():
        kv_block(pl.multiple_of((i - 1) * tq, tq), tq)

    acc_lo, acc_hi = acc0_scr[...], acc1_scr[...]
    o_lo = acc_lo / acc_lo[:, LANES - 1:LANES]
    o_hi = acc_hi / acc_hi[:, 0:1]
    o_ref[...] = jnp.where(lane_lo, o_lo, o_hi).astype(BF16)


def _fox_attn(q, kt, v, cum4, nh):
    bsz, seq, d = q.shape
    hd = d // nh
    tq = FX_TQ
    assert seq % tq == 0 and 2 * hd == LANES
    return pl.pallas_call(
        functools.partial(_fxattn_kernel, tq=tq, hd=hd),
        grid=(bsz, nh // 2, seq // tq),
        in_specs=[
            pl.BlockSpec((None, tq, LANES), lambda b, p, i: (b, i, p)),
            pl.BlockSpec((None, LANES, seq), lambda b, p, i: (b, p, 0)),
            pl.BlockSpec((None, seq, LANES), lambda b, p, i: (b, 0, p)),
            pl.BlockSpec((None, None, 2, seq), lambda b, p, i: (b, p, 0, 0)),
        ],
        out_specs=pl.BlockSpec((None, tq, LANES), lambda b, p, i: (b, i, p)),
        out_shape=jax.ShapeDtypeStruct((bsz, seq, d), BF16),
        scratch_shapes=[pltpu.VMEM((tq, LANES), F32)] * 4,
        compiler_params=_params("parallel", "parallel", "arbitrary"),
        name="foxattn",
    )(q, kt, v, cum4)


def kernel(x, ffn_norm, ffn_w_in, ffn_w_out, mix_norm, rg_w_in, rg_conv_w, rg_conv_b, rg_w_a, rg_b_a, rg_w_x,
           rg_b_x, rg_lambda, rg_w_out, fox_w_in, fox_b_f, fox_q_norm, fox_k_norm, fox_w_out, cv_w_in, cv_b_in,
           cv_dw_w, cv_dw_b, cv_ln_g, cv_ln_b, cv_w_out, cv_b_out):
    bsz, seq, d = x.shape
    depth = ffn_norm.shape[0]
    nh = fox_b_f.shape[-1]
    hd = fox_q_norm.shape[-1]
    assert nh * hd == d and nh <= LANES

    ffn_w_in_b = ffn_w_in.astype(BF16)
    ffn_w_out_b = ffn_w_out.astype(BF16)
    rg_w_in_b, rg_w_a_b, rg_w_x_b, rg_w_out_b = (w.astype(BF16) for w in (rg_w_in, rg_w_a, rg_w_x, rg_w_out))
    fox_w_qkv_b = fox_w_in[:, :, :3 * d].astype(BF16)
    fox_w_f_b = jnp.pad(fox_w_in[:, :, 3 * d:], ((0, 0), (0, 0), (0, LANES - nh))).astype(BF16)
    fox_w_out_b = fox_w_out.astype(BF16)
    cv_w_in_b, cv_w_out_b = cv_w_in.astype(BF16), cv_w_out.astype(BF16)

    def row(p):
        return p[None, :]

    def pad_lanes(p):
        return jnp.pad(p, (0, LANES - p.shape[0]))[None, :]

    def twice(p):
        return jnp.concatenate([p, p])[None, :]

    for i in range(depth):
        kind, j = i % N_MIXERS, i // N_MIXERS
        x = _ffn(x.reshape(bsz * seq, d), row(ffn_norm[i, 0]), ffn_w_in_b, ffn_w_out_b, (i, 0))
        x = x.reshape(bsz, seq, d)
        g = row(mix_norm[i])
        proj = None
        if kind == 0:
            x = _rglru(x, g, rg_w_in_b, rg_conv_w[j], row(rg_conv_b[j]), rg_w_a_b, row(rg_b_a[j]), rg_w_x_b,
                       row(rg_b_x[j]), row(rg_lambda[j]), rg_w_out_b, (j,))
        elif kind == 1:
            q, kt, v, cum = _fox_proj(x, g, fox_w_qkv_b, fox_w_f_b, pad_lanes(fox_b_f[j]), twice(fox_q_norm[j]),
                                     twice(fox_k_norm[j]), (j,), nh)
            o = _fox_attn(q, kt, v, cum.reshape(bsz, nh // 2, 2, seq), nh)
            proj = (o.reshape(bsz * seq, d), fox_w_out_b, (j,))
        else:
            x = _conv_module(x, g, cv_w_in_b, row(cv_b_in[j]), cv_dw_w[j], row(cv_dw_b[j]), row(cv_ln_g[j]),
                             row(cv_ln_b[j]), cv_w_out_b, row(cv_b_out[j]), (j,))
        x = _ffn(x.reshape(bsz * seq, d), row(ffn_norm[i, 1]), ffn_w_in_b, ffn_w_out_b, (i, 1), proj)
        x = x.reshape(bsz, seq, d)
    return x
```

```python
import functools

import jax
import jax.numpy as jnp
from jax import lax
from jax.experimental import pallas as pl
from jax.experimental.pallas import tpu as pltpu

F32 = jnp.float32
BF16 = jnp.bfloat16

EPS = 1e-6
N_MIXERS = 3
RG_C = 8.0

LANES = 128
SUBLANES = 8
VMEM_LIMIT_BYTES = 56 * 1024 * 1024

NEG_BIG = -1e30
LOG2E = 1.4426950408889634


def _dot(a, b):
    return jnp.dot(a, b, preferred_element_type=F32)


def _sigmoid(x):
    return 1.0 / (1.0 + jnp.exp(-x))


def _rms(x, g):
    ms = jnp.mean(x * x, axis=-1, keepdims=True)
    return x * lax.rsqrt(ms + EPS) * g


def _softplus(z):
    return jnp.maximum(z, 0.0) + jnp.log1p(jnp.exp(-jnp.abs(z)))


def _params(*semantics):
    return pltpu.CompilerParams(dimension_semantics=semantics, vmem_limit_bytes=VMEM_LIMIT_BYTES)


def _const_spec(shape):
    n = len(shape)
    return pl.BlockSpec(shape, lambda *_: (0,) * n, pipeline_mode=pl.Buffered(1))


def _layer_spec(shape, lead):
    n = len(shape)
    block = (None,) * len(lead) + tuple(shape)
    return pl.BlockSpec(block, lambda *_: tuple(lead) + (0,) * n, pipeline_mode=pl.Buffered(1))


FFN_TM = 1024
FFN_TF = 256


def _swiglu_step(x, g_ref, win_ref, wout_ref, out_ref, d_ff, tf):
    h = _rms(x, g_ref[...]).astype(BF16)
    acc = None
    for c in range(d_ff // tf):
        gate = _dot(h, win_ref[:, c * tf:(c + 1) * tf])
        up = _dot(h, win_ref[:, d_ff + c * tf:d_ff + (c + 1) * tf])
        a = (gate * _sigmoid(gate) * up).astype(BF16)
        part = _dot(a, wout_ref[c * tf:(c + 1) * tf, :])
        acc = part if acc is None else acc + part
    out_ref[...] = x + 0.5 * acc


def _ffn_kernel(x_ref, g_ref, win_ref, wout_ref, out_ref, *, d_ff, tf):
    _swiglu_step(x_ref[...], g_ref, win_ref, wout_ref, out_ref, d_ff, tf)


def _proj_ffn_kernel(x_ref, o_ref, wo_ref, g_ref, win_ref, wout_ref, out_ref, *, d_ff, tf):
    x = x_ref[...] + _dot(o_ref[...], wo_ref[...])
    _swiglu_step(x, g_ref, win_ref, wout_ref, out_ref, d_ff, tf)


def _ffn(x2, g, w_in, w_out, lead, proj=None):
    t, d = x2.shape
    d_ff = w_out.shape[-2]
    assert t % FFN_TM == 0 and d_ff % FFN_TF == 0
    tile = pl.BlockSpec((FFN_TM, d), lambda i: (i, 0))
    ffn_specs = [_const_spec((1, d)), _layer_spec((d, 2 * d_ff), lead), _layer_spec((d_ff, d), lead)]
    if proj is None:
        body, in_specs, args = _ffn_kernel, [tile] + ffn_specs, (x2, g, w_in, w_out)
    else:
        o2, wo, wo_lead = proj
        body = _proj_ffn_kernel
        in_specs = [tile, tile, _layer_spec((d, d), wo_lead)] + ffn_specs
        args = (x2, o2, wo, g, w_in, w_out)
    return pl.pallas_call(
        functools.partial(body, d_ff=d_ff, tf=FFN_TF),
        grid=(t // FFN_TM,),
        in_specs=in_specs,
        out_specs=tile,
        out_shape=jax.ShapeDtypeStruct((t, d), F32),
        compiler_params=_params("parallel"),
        name="ffn" if proj is None else "projffn",
    )(*args)


RG_TS = 512
RG_SUB = 256


def _rg_kernel(x_ref, g_ref, perm_ref, unperm_ref, win_ref, cw_ref, cb_ref, wa_ref, ba_ref, wx_ref, bx_ref, lam_ref,
               wout_ref, o_ref, tail_scr, hcar_scr, *, ts, sub, d, conv_w, npair, pblk):
    ngrp = sub // SUBLANES
    nhist = conv_w - 1
    nsub = ts // sub

    @pl.when(pl.program_id(1) == 0)
    def _():
        tail_scr[...] = jnp.zeros_like(tail_scr)
        hcar_scr[...] = jnp.zeros_like(hcar_scr)

    def grp(t, g):
        return t[g * SUBLANES:(g + 1) * SUBLANES, :]

    row = lax.broadcasted_iota(jnp.int32, (SUBLANES, d), 0)
    zscale = RG_C * _softplus(-lam_ref[...])

    xs, gates, us = [], [], []
    for i in range(nsub):
        x = x_ref[i * sub:(i + 1) * sub, :]
        hp = _dot(perm_ref[...], _rms(x, g_ref[...]).astype(BF16)).astype(BF16)
        xs.append(x)
        gates.append(_dot(hp, win_ref[:, 0:d]))
        us.append(_dot(hp, win_ref[:, d:2 * d]))

    ys = []
    prev_last = tail_scr[...]
    carry = hcar_scr[...]
    for i in range(nsub):
        u = us[i]
        last = u[sub - nhist * SUBLANES:sub, :]
        pre = [jnp.where(row == 0, pltpu.roll(grp(prev_last, k), 1, axis=0), pltpu.roll(grp(last, k), 1, axis=0))
               for k in range(nhist)]
        prev_last = last
        ext = jnp.concatenate(pre + [u], axis=0)
        uc = u * cw_ref[conv_w - 1:conv_w, :] + cb_ref[...]
        for dly in range(1, conv_w):
            lo = (nhist - dly) * SUBLANES
            uc = uc + ext[lo:lo + sub, :] * cw_ref[conv_w - 1 - dly:conv_w - dly, :]

        ub = uc.astype(BF16)
        ra = jnp.concatenate([_dot(ub[:, n * pblk:(n + 1) * pblk], wa_ref[n]) for n in range(npair)], axis=-1)
        rx = jnp.concatenate([_dot(ub[:, n * pblk:(n + 1) * pblk], wx_ref[n]) for n in range(npair)], axis=-1)
        r = _sigmoid(ra + ba_ref[...])
        ig = _sigmoid(rx + bx_ref[...])
        z = r * zscale
        a = jnp.exp(-z)
        b = jnp.sqrt(jnp.tanh(z) * (a * a + 1.0)) * (ig * uc)

        hl, al = [grp(b, 0)], [grp(a, 0)]
        for g in range(1, ngrp):
            ag = grp(a, g)
            hl.append(ag * hl[-1] + grp(b, g))
            al.append(ag * al[-1])
        ea, eh = al[-1], hl[-1]
        for sh in (1, 2, 4):
            keep = row >= sh
            ea_s = jnp.where(keep, pltpu.roll(ea, sh, axis=0), 1.0)
            eh_s = jnp.where(keep, pltpu.roll(eh, sh, axis=0), 0.0)
            eh = ea * eh_s + eh
            ea = ea * ea_s
        ends = ea * carry + eh
        cin = jnp.where(row == 0, carry, pltpu.roll(ends, 1, axis=0))
        carry = ends[SUBLANES - 1:SUBLANES, :]
        hs = jnp.concatenate([hl[g] + al[g] * cin for g in range(ngrp)], axis=0)
        ys.append((hs * jax.nn.gelu(gates[i])).astype(BF16))
    tail_scr[...] = prev_last
    hcar_scr[...] = carry

    for i in range(nsub):
        yn = _dot(unperm_ref[...], ys[i]).astype(BF16)
        o_ref[i * sub:(i + 1) * sub, :] = xs[i] + _dot(yn, wout_ref[...])


def _time_perm(n):
    r = jnp.arange(n)
    t = (r % SUBLANES) * (n // SUBLANES) + r // SUBLANES
    return (t[:, None] == jnp.arange(n)[None, :]).astype(BF16)


def _pair_blocks(w):
    l, nb, c, _ = w.shape
    w = w.reshape(l, nb // 2, 2, c, c)
    z = jnp.zeros_like(w[:, :, 0])
    top = jnp.concatenate([w[:, :, 0], z], axis=-1)
    bot = jnp.concatenate([z, w[:, :, 1]], axis=-1)
    return jnp.concatenate([top, bot], axis=-2)


def _rglru(x, g, w_in, conv_w, conv_b, w_a2, b_a, w_x2, b_x, lam, w_out, lead):
    bsz, seq, d = x.shape
    kw = conv_w.shape[-2]
    npair, pblk = w_a2.shape[-3], w_a2.shape[-1]
    ts, sub = RG_TS, RG_SUB
    assert seq % ts == 0 and ts % sub == 0 and kw - 1 <= sub // SUBLANES and npair * pblk == d
    perm = _time_perm(sub)
    return pl.pallas_call(
        functools.partial(_rg_kernel, ts=ts, sub=sub, d=d, conv_w=kw, npair=npair, pblk=pblk),
        grid=(bsz, seq // ts),
        in_specs=[
            pl.BlockSpec((None, ts, d), lambda b, s: (b, s, 0)),
            _const_spec((1, d)),
            _const_spec((sub, sub)),
            _const_spec((sub, sub)),
            _layer_spec((d, 2 * d), lead),
            _const_spec((kw, d)),
            _const_spec((1, d)),
            _layer_spec((npair, pblk, pblk), lead),
            _const_spec((1, d)),
            _layer_spec((npair, pblk, pblk), lead),
            _const_spec((1, d)),
            _const_spec((1, d)),
            _layer_spec((d, d), lead),
        ],
        out_specs=pl.BlockSpec((None, ts, d), lambda b, s: (b, s, 0)),
        out_shape=jax.ShapeDtypeStruct((bsz, seq, d), F32),
        scratch_shapes=[
            pltpu.VMEM(((kw - 1) * SUBLANES, d), F32),
            pltpu.VMEM((1, d), F32),
        ],
        compiler_params=_params("parallel", "arbitrary"),
        name="rglru",
    )(x, g, perm, perm.T, w_in, conv_w, conv_b, w_a2, b_a, w_x2, b_x, lam, w_out)


CV_TS = 256
CV_ROWS = 128
CV_HIST = 32


def _cv_kernel(x_ref, g_ref, win_ref, bin_ref, dw_ref, dwb_ref, lng_ref, lnb_ref, wout_ref, bout_ref,
               o_ref, ext_scr, y_scr, *, ts, d, kw):
    s = pl.program_id(1)

    @pl.when(s == 0)
    def _():
        ext_scr[0:CV_HIST, :] = jnp.zeros((CV_HIST, d), F32)

    x = x_ref[...]
    h = _rms(x, g_ref[...]).astype(BF16)
    val = _dot(h, win_ref[:, 0:d]) + bin_ref[:, 0:d]
    gate = _dot(h, win_ref[:, d:2 * d]) + bin_ref[:, d:2 * d]
    hh = val * _sigmoid(gate)
    ext_scr[CV_HIST:CV_HIST + ts, :] = hh

    def col_block(c, carry):
        c0 = pl.multiple_of(c * LANES, LANES)
        w = dw_ref[:, pl.ds(c0, LANES)]
        bias = dwb_ref[:, pl.ds(c0, LANES)]
        for r0 in range(0, ts, CV_ROWS):
            slab = ext_scr[r0:r0 + CV_ROWS + CV_HIST, pl.ds(c0, LANES)]
            acc = jnp.zeros((CV_ROWS, LANES), F32) + bias
            for r in range(SUBLANES):
                rolled = slab if r == 0 else pltpu.roll(slab, r, axis=0)
                for a in range(CV_HIST // SUBLANES):
                    dly = SUBLANES * a + r
                    if dly < kw:
                        lo = CV_HIST - SUBLANES * a
                        acc = acc + rolled[lo:lo + CV_ROWS, :] * w[kw - 1 - dly:kw - dly, :]
            y_scr[r0:r0 + CV_ROWS, pl.ds(c0, LANES)] = acc
        return carry

    lax.fori_loop(0, d // LANES, col_block, 0)
    ext_scr[0:CV_HIST, :] = hh[ts - CV_HIST:ts, :]

    y = y_scr[...]
    mu = jnp.mean(y, axis=-1, keepdims=True)
    yc = y - mu
    var = jnp.mean(yc * yc, axis=-1, keepdims=True)
    z = yc * lax.rsqrt(var + EPS) * lng_ref[...] + lnb_ref[...]
    z = (z * _sigmoid(z)).astype(BF16)
    o_ref[...] = x + _dot(z, wout_ref[...]) + bout_ref[...]


def _conv_module(x, g, w_in, b_in, dw_w, dw_b, ln_g, ln_b, w_out, b_out, lead):
    bsz, seq, d = x.shape
    kw = dw_w.shape[-2]
    ts = CV_TS
    assert seq % ts == 0 and kw - 1 <= CV_HIST and ts % CV_ROWS == 0
    return pl.pallas_call(
        functools.partial(_cv_kernel, ts=ts, d=d, kw=kw),
        grid=(bsz, seq // ts),
        in_specs=[
            pl.BlockSpec((None, ts, d), lambda b, s: (b, s, 0)),
            _const_spec((1, d)),
            _layer_spec((d, 2 * d), lead),
            _const_spec((1, 2 * d)),
            _const_spec((kw, d)),
            _const_spec((1, d)),
            _const_spec((1, d)),
            _const_spec((1, d)),
            _layer_spec((d, d), lead),
            _const_spec((1, d)),
        ],
        out_specs=pl.BlockSpec((None, ts, d), lambda b, s: (b, s, 0)),
        out_shape=jax.ShapeDtypeStruct((bsz, seq, d), F32),
        scratch_shapes=[
            pltpu.VMEM((ts + CV_HIST, d), F32),
            pltpu.VMEM((ts, d), F32),
        ],
        compiler_params=_params("parallel", "arbitrary"),
        name="convmod",
    )(x, g, w_in, b_in, dw_w, dw_b, ln_g, ln_b, w_out, b_out)


FX_TS = 512
FX_TQ = 512


def _head_rms(t, g2, lane_lo, hd):
    sq = t * t
    s_lo = jnp.sum(jnp.where(lane_lo, sq, 0.0), axis=-1, keepdims=True)
    s_hi = jnp.sum(jnp.where(lane_lo, 0.0, sq), axis=-1, keepdims=True)
    ms = jnp.where(lane_lo, s_lo, s_hi) * (1.0 / hd)
    return t * lax.rsqrt(ms + EPS) * g2


def _fxproj_kernel(x_ref, g_ref, wqkv_ref, wf_ref, bf_ref, qg_ref, kg_ref,
                   q_ref, kt_ref, v_ref, cum_ref, car_scr, *, ts, d, hd, nh):
    s = pl.program_id(1)

    @pl.when(s == 0)
    def _():
        car_scr[...] = jnp.zeros_like(car_scr)

    x = x_ref[...]
    h = _rms(x, g_ref[...]).astype(BF16)
    lane_lo = lax.broadcasted_iota(jnp.int32, (ts, LANES), 1) < hd
    scale = hd ** -0.5 * LOG2E
    q = _dot(h, wqkv_ref[:, 0:d])
    k = _dot(h, wqkv_ref[:, d:2 * d])
    for c in range(d // LANES):
        cols = slice(c * LANES, (c + 1) * LANES)
        q_ref[:, cols] = (_head_rms(q[:, cols], qg_ref[...], lane_lo, hd) * scale).astype(BF16)
        kt_ref[cols, :] = jnp.transpose(_head_rms(k[:, cols], kg_ref[...], lane_lo, hd)).astype(BF16)
    v_ref[...] = _dot(h, wqkv_ref[:, 2 * d:3 * d]).astype(BF16)

    f = _dot(h, wf_ref[...]) + bf_ref[...]
    cum = jnp.minimum(f, 0.0) - jnp.log1p(jnp.exp(-jnp.abs(f)))
    row = lax.broadcasted_iota(jnp.int32, (ts, LANES), 0)
    sh = 1
    while sh < ts:
        cum = cum + jnp.where(row >= sh, pltpu.roll(cum, sh, axis=0), 0.0)
        sh *= 2
    cum = cum + car_scr[...]
    car_scr[...] = cum[ts - 1:ts, :]
    cum_ref[...] = jnp.transpose(cum * LOG2E)[0:nh, :]


def _fox_proj(x, g, w_qkv, w_f, b_f, q_g2, k_g2, lead, nh):
    bsz, seq, d = x.shape
    hd = d // nh
    ts = FX_TS
    assert seq % ts == 0 and 2 * hd == LANES and nh % SUBLANES == 0
    tile = pl.BlockSpec((None, ts, d), lambda b, s: (b, s, 0))
    return pl.pallas_call(
        functools.partial(_fxproj_kernel, ts=ts, d=d, hd=hd, nh=nh),
        grid=(bsz, seq // ts),
        in_specs=[
            tile,
            _const_spec((1, d)),
            _layer_spec((d, 3 * d), lead),
            _layer_spec((d, LANES), lead),
            _const_spec((1, LANES)),
            _const_spec((1, LANES)),
            _const_spec((1, LANES)),
        ],
        out_specs=[tile, pl.BlockSpec((None, d, ts), lambda b, s: (b, 0, s)), tile,
                   pl.BlockSpec((None, nh, ts), lambda b, s: (b, 0, s))],
        out_shape=[
            jax.ShapeDtypeStruct((bsz, seq, d), BF16),
            jax.ShapeDtypeStruct((bsz, d, seq), BF16),
            jax.ShapeDtypeStruct((bsz, seq, d), BF16),
            jax.ShapeDtypeStruct((bsz, nh, seq), F32),
        ],
        scratch_shapes=[pltpu.VMEM((1, LANES), F32)],
        compiler_params=_params("parallel", "arbitrary"),
        name="foxproj",
    )(x, g, w_qkv, w_f, b_f, q_g2, k_g2)


def _fxattn_kernel(q_ref, kt_ref, v_ref, cum_ref, o_ref, m0_scr, m1_scr, acc0_scr, acc1_scr, *, tq, hd):
    i = pl.program_id(2)
    q = q_ref[...]
    lane_lo = lax.broadcasted_iota(jnp.int32, (tq, LANES), 1) < hd
    zero = jnp.zeros_like(q)
    qs = (jnp.where(lane_lo, q, zero), jnp.where(lane_lo, zero, q))
    m_scrs = (m0_scr, m1_scr)
    acc_scrs = (acc0_scr, acc1_scr)

    def scores(hh, j0, tk):
        return _dot(qs[hh], kt_ref[:, pl.ds(j0, tk)]) - cum_ref[hh:hh + 1, pl.ds(j0, tk)]

    def values(j0, tk):
        vb = v_ref[pl.ds(j0, tk), :]
        lo = lax.broadcasted_iota(jnp.int32, (tk, LANES), 1) < hd
        one = jnp.ones_like(vb)
        return (jnp.where(lo, vb, one), jnp.where(lo, one, vb))

    def row_max(sc):
        return jnp.broadcast_to(jnp.max(sc, axis=-1, keepdims=True), (tq, LANES))

    def lanes_to(m, tk):
        return jnp.concatenate([m] * (tk // LANES), axis=1)

    d0 = pl.multiple_of(i * tq, tq)
    half = tq // 2
    parts = [(r0, tk, hh) for r0, tk in ((0, half), (half, tq)) for hh in range(2)]
    vss = {tk: values(d0, tk) for tk in (half, tq)}
    scs = []
    for r0, tk, hh in parts:
        causal = (lax.broadcasted_iota(jnp.int32, (half, tk), 0) + r0
                  >= lax.broadcasted_iota(jnp.int32, (half, tk), 1))
        sc = _dot(qs[hh][r0:r0 + half, :], kt_ref[:, pl.ds(d0, tk)]) - cum_ref[hh:hh + 1, pl.ds(d0, tk)]
        scs.append(jnp.where(causal, sc, NEG_BIG))
    ms = [jnp.broadcast_to(jnp.max(sc, axis=-1, keepdims=True), (half, LANES)) for sc in scs]
    ps = [jnp.exp2(sc - lanes_to(m, tk)).astype(BF16) for sc, m, (r0, tk, hh) in zip(scs, ms, parts)]
    for p, m, (r0, tk, hh) in zip(ps, ms, parts):
        m_scrs[hh][r0:r0 + half, :] = m
        acc_scrs[hh][r0:r0 + half, :] = _dot(p, vss[tk][hh])

    def kv_block(j0, tk):
        vs = values(j0, tk)
        scs = [scores(hh, j0, tk) for hh in range(2)]
        m_olds = [m_scrs[hh][...] for hh in range(2)]
        m_news = [jnp.maximum(m_olds[hh], row_max(scs[hh])) for hh in range(2)]
        ps = [jnp.exp2(scs[hh] - lanes_to(m_news[hh], tk)).astype(BF16) for hh in range(2)]
        for hh in range(2):
            m_scrs[hh][...] = m_news[hh]
            acc_scrs[hh][...] = jnp.exp2(m_olds[hh] - m_news[hh]) * acc_scrs[hh][...] + _dot(ps[hh], vs[hh])

    def wide_block(j, carry):
        kv_block(pl.multiple_of(j * 2 * tq, 2 * tq), 2 * tq)
        return carry

    lax.fori_loop(0, i // 2, wide_block, 0)

    @pl.when(i % 2 == 1)
    def _():
        kv_block(pl.multiple_of((i - 1) * tq, tq), tq)

    acc_lo, acc_hi = acc0_scr[...], acc1_scr[...]
    o_lo = acc_lo / acc_lo[:, LANES - 1:LANES]
    o_hi = acc_hi / acc_hi[:, 0:1]
    o_ref[...] = jnp.where(lane_lo, o_lo, o_hi).astype(BF16)


def _fox_attn(q, kt, v, cum4, nh):
    bsz, seq, d = q.shape
    hd = d // nh
    tq = FX_TQ
    assert seq % tq == 0 and 2 * hd == LANES
    return pl.pallas_call(
        functools.partial(_fxattn_kernel, tq=tq, hd=hd),
        grid=(bsz, nh // 2, seq // tq),
        in_specs=[
            pl.BlockSpec((None, tq, LANES), lambda b, p, i: (b, i, p)),
            pl.BlockSpec((None, LANES, seq), lambda b, p, i: (b, p, 0)),
            pl.BlockSpec((None, seq, LANES), lambda b, p, i: (b, 0, p)),
            pl.BlockSpec((None, None, 2, seq), lambda b, p, i: (b, p, 0, 0)),
        ],
        out_specs=pl.BlockSpec((None, tq, LANES), lambda b, p, i: (b, i, p)),
        out_shape=jax.ShapeDtypeStruct((bsz, seq, d), BF16),
        scratch_shapes=[pltpu.VMEM((tq, LANES), F32)] * 4,
        compiler_params=_params("parallel", "parallel", "arbitrary"),
        name="foxattn",
    )(q, kt, v, cum4)


def kernel(x, ffn_norm, ffn_w_in, ffn_w_out, mix_norm, rg_w_in, rg_conv_w, rg_conv_b, rg_w_a, rg_b_a, rg_w_x,
           rg_b_x, rg_lambda, rg_w_out, fox_w_in, fox_b_f, fox_q_norm, fox_k_norm, fox_w_out, cv_w_in, cv_b_in,
           cv_dw_w, cv_dw_b, cv_ln_g, cv_ln_b, cv_w_out, cv_b_out):
    bsz, seq, d = x.shape
    depth = ffn_norm.shape[0]
    nh = fox_b_f.shape[-1]
    hd = fox_q_norm.shape[-1]
    assert nh * hd == d and nh <= LANES

    ffn_w_in_b = ffn_w_in.astype(BF16)
    ffn_w_out_b = ffn_w_out.astype(BF16)
    rg_w_in_b, rg_w_out_b = rg_w_in.astype(BF16), rg_w_out.astype(BF16)
    rg_w_a_b, rg_w_x_b = _pair_blocks(rg_w_a.astype(BF16)), _pair_blocks(rg_w_x.astype(BF16))
    fox_w_qkv_b = fox_w_in[:, :, :3 * d].astype(BF16)
    fox_w_f_b = jnp.pad(fox_w_in[:, :, 3 * d:], ((0, 0), (0, 0), (0, LANES - nh))).astype(BF16)
    fox_w_out_b = fox_w_out.astype(BF16)
    cv_w_in_b, cv_w_out_b = cv_w_in.astype(BF16), cv_w_out.astype(BF16)

    def row(p):
        return p[None, :]

    def pad_lanes(p):
        return jnp.pad(p, (0, LANES - p.shape[0]))[None, :]

    def twice(p):
        return jnp.concatenate([p, p])[None, :]

    for i in range(depth):
        kind, j = i % N_MIXERS, i // N_MIXERS
        x = _ffn(x.reshape(bsz * seq, d), row(ffn_norm[i, 0]), ffn_w_in_b, ffn_w_out_b, (i, 0))
        x = x.reshape(bsz, seq, d)
        g = row(mix_norm[i])
        proj = None
        if kind == 0:
            x = _rglru(x, g, rg_w_in_b, rg_conv_w[j], row(rg_conv_b[j]), rg_w_a_b, row(rg_b_a[j]), rg_w_x_b,
                       row(rg_b_x[j]), row(rg_lambda[j]), rg_w_out_b, (j,))
        elif kind == 1:
            q, kt, v, cum = _fox_proj(x, g, fox_w_qkv_b, fox_w_f_b, pad_lanes(fox_b_f[j]), twice(fox_q_norm[j]),
                                     twice(fox_k_norm[j]), (j,), nh)
            o = _fox_attn(q, kt, v, cum.reshape(bsz, nh // 2, 2, seq), nh)
            proj = (o.reshape(bsz * seq, d), fox_w_out_b, (j,))
        else:
            x = _conv_module(x, g, cv_w_in_b, row(cv_b_in[j]), cv_dw_w[j], row(cv_dw_b[j]), row(cv_ln_g[j]),
                             row(cv_ln_b[j]), cv_w_out_b, row(cv_b_out[j]), (j,))
        x = _ffn(x.reshape(bsz * seq, d), row(ffn_norm[i, 1]), ffn_w_in_b, ffn_w_out_b, (i, 1), proj)
        x = x.reshape(bsz, seq, d)
    return x
```

```python
import functools

import jax
import jax.numpy as jnp
from jax import lax
from jax.experimental import pallas as pl
from jax.experimental.pallas import tpu as pltpu

F32 = jnp.float32
BF16 = jnp.bfloat16

EPS = 1e-6
N_MIXERS = 3
RG_C = 8.0

LANES = 128
SUBLANES = 8
VMEM_LIMIT_BYTES = 56 * 1024 * 1024

NEG_BIG = -1e30
LOG2E = 1.4426950408889634


def _dot(a, b):
    return jnp.dot(a, b, preferred_element_type=F32)


def _sigmoid(x):
    return 1.0 / (1.0 + jnp.exp(-x))


def _rms(x, g):
    ms = jnp.mean(x * x, axis=-1, keepdims=True)
    return x * lax.rsqrt(ms + EPS) * g


def _softplus(z):
    return jnp.maximum(z, 0.0) + jnp.log1p(jnp.exp(-jnp.abs(z)))


def _params(*semantics):
    return pltpu.CompilerParams(dimension_semantics=semantics, vmem_limit_bytes=VMEM_LIMIT_BYTES)


def _const_spec(shape):
    n = len(shape)
    return pl.BlockSpec(shape, lambda *_: (0,) * n, pipeline_mode=pl.Buffered(1))


def _layer_spec(shape, lead):
    n = len(shape)
    block = (None,) * len(lead) + tuple(shape)
    return pl.BlockSpec(block, lambda *_: tuple(lead) + (0,) * n, pipeline_mode=pl.Buffered(1))


FFN_TM = 1024
FFN_TF = 256


def _swiglu_step(x, g_ref, win_ref, wout_ref, out_ref, d_ff, tf):
    h = _rms(x, g_ref[...]).astype(BF16)
    acc = None
    for c in range(d_ff // tf):
        gate = _dot(h, win_ref[:, c * tf:(c + 1) * tf])
        up = _dot(h, win_ref[:, d_ff + c * tf:d_ff + (c + 1) * tf])
        a = (gate * _sigmoid(gate) * up).astype(BF16)
        part = _dot(a, wout_ref[c * tf:(c + 1) * tf, :])
        acc = part if acc is None else acc + part
    out_ref[...] = x + 0.5 * acc


def _ffn_kernel(x_ref, g_ref, win_ref, wout_ref, out_ref, *, d_ff, tf):
    _swiglu_step(x_ref[...], g_ref, win_ref, wout_ref, out_ref, d_ff, tf)


def _proj_ffn_kernel(x_ref, o_ref, wo_ref, g_ref, win_ref, wout_ref, out_ref, *, d_ff, tf):
    x = x_ref[...] + _dot(o_ref[...], wo_ref[...])
    _swiglu_step(x, g_ref, win_ref, wout_ref, out_ref, d_ff, tf)


def _ffn(x2, g, w_in, w_out, lead, proj=None):
    t, d = x2.shape
    d_ff = w_out.shape[-2]
    assert t % FFN_TM == 0 and d_ff % FFN_TF == 0
    tile = pl.BlockSpec((FFN_TM, d), lambda i: (i, 0))
    ffn_specs = [_const_spec((1, d)), _layer_spec((d, 2 * d_ff), lead), _layer_spec((d_ff, d), lead)]
    if proj is None:
        body, in_specs, args = _ffn_kernel, [tile] + ffn_specs, (x2, g, w_in, w_out)
    else:
        o2, wo, wo_lead = proj
        body = _proj_ffn_kernel
        in_specs = [tile, tile, _layer_spec((d, d), wo_lead)] + ffn_specs
        args = (x2, o2, wo, g, w_in, w_out)
    return pl.pallas_call(
        functools.partial(body, d_ff=d_ff, tf=FFN_TF),
        grid=(t // FFN_TM,),
        in_specs=in_specs,
        out_specs=tile,
        out_shape=jax.ShapeDtypeStruct((t, d), F32),
        compiler_params=_params("parallel"),
        name="ffn" if proj is None else "projffn",
    )(*args)


RG_TS = 1024
RG_SUB = 256


def _rg_kernel(x_ref, g_ref, perm_ref, unperm_ref, win_ref, cw_ref, cb_ref, wa_ref, ba_ref, wx_ref, bx_ref, lam_ref,
               wout_ref, o_ref, tail_scr, hcar_scr, *, ts, sub, d, conv_w, npair, pblk):
    ngrp = sub // SUBLANES
    nhist = conv_w - 1
    nsub = ts // sub

    @pl.when(pl.program_id(1) == 0)
    def _():
        tail_scr[...] = jnp.zeros_like(tail_scr)
        hcar_scr[...] = jnp.zeros_like(hcar_scr)

    def grp(t, g):
        return t[g * SUBLANES:(g + 1) * SUBLANES, :]

    row = lax.broadcasted_iota(jnp.int32, (SUBLANES, d), 0)
    zscale = RG_C * _softplus(-lam_ref[...])

    xs, gates, us = [], [], []
    for i in range(nsub):
        x = x_ref[i * sub:(i + 1) * sub, :]
        hp = _dot(perm_ref[...], _rms(x, g_ref[...]).astype(BF16)).astype(BF16)
        xs.append(x)
        gates.append(_dot(hp, win_ref[:, 0:d]))
        us.append(_dot(hp, win_ref[:, d:2 * d]))

    ys = []
    prev_last = tail_scr[...]
    carry = hcar_scr[...]
    for i in range(nsub):
        u = us[i]
        last = u[sub - nhist * SUBLANES:sub, :]
        pre = [jnp.where(row == 0, pltpu.roll(grp(prev_last, k), 1, axis=0), pltpu.roll(grp(last, k), 1, axis=0))
               for k in range(nhist)]
        prev_last = last
        ext = jnp.concatenate(pre + [u], axis=0)
        uc = u * cw_ref[conv_w - 1:conv_w, :] + cb_ref[...]
        for dly in range(1, conv_w):
            lo = (nhist - dly) * SUBLANES
            uc = uc + ext[lo:lo + sub, :] * cw_ref[conv_w - 1 - dly:conv_w - dly, :]

        ub = uc.astype(BF16)
        ra = jnp.concatenate([_dot(ub[:, n * pblk:(n + 1) * pblk], wa_ref[n]) for n in range(npair)], axis=-1)
        rx = jnp.concatenate([_dot(ub[:, n * pblk:(n + 1) * pblk], wx_ref[n]) for n in range(npair)], axis=-1)
        r = _sigmoid(ra + ba_ref[...])
        ig = _sigmoid(rx + bx_ref[...])
        z = r * zscale
        a = jnp.exp(-z)
        b = jnp.sqrt(jnp.tanh(z) * (a * a + 1.0)) * (ig * uc)

        hl, al = [grp(b, 0)], [grp(a, 0)]
        for g in range(1, ngrp):
            ag = grp(a, g)
            hl.append(ag * hl[-1] + grp(b, g))
            al.append(ag * al[-1])
        ea, eh = al[-1], hl[-1]
        for sh in (1, 2, 4):
            keep = row >= sh
            ea_s = jnp.where(keep, pltpu.roll(ea, sh, axis=0), 1.0)
            eh_s = jnp.where(keep, pltpu.roll(eh, sh, axis=0), 0.0)
            eh = ea * eh_s + eh
            ea = ea * ea_s
        ends = ea * carry + eh
        cin = jnp.where(row == 0, carry, pltpu.roll(ends, 1, axis=0))
        carry = ends[SUBLANES - 1:SUBLANES, :]
        hs = jnp.concatenate([hl[g] + al[g] * cin for g in range(ngrp)], axis=0)
        ys.append((hs * jax.nn.gelu(gates[i])).astype(BF16))
    tail_scr[...] = prev_last
    hcar_scr[...] = carry

    for i in range(nsub):
        yn = _dot(unperm_ref[...], ys[i]).astype(BF16)
        o_ref[i * sub:(i + 1) * sub, :] = xs[i] + _dot(yn, wout_ref[...])


def _time_perm(n):
    r = jnp.arange(n)
    t = (r % SUBLANES) * (n // SUBLANES) + r // SUBLANES
    return (t[:, None] == jnp.arange(n)[None, :]).astype(BF16)


def _pair_blocks(w):
    l, nb, c, _ = w.shape
    w = w.reshape(l, nb // 2, 2, c, c)
    z = jnp.zeros_like(w[:, :, 0])
    top = jnp.concatenate([w[:, :, 0], z], axis=-1)
    bot = jnp.concatenate([z, w[:, :, 1]], axis=-1)
    return jnp.concatenate([top, bot], axis=-2)


def _rglru(x, g, w_in, conv_w, conv_b, w_a2, b_a, w_x2, b_x, lam, w_out, lead):
    bsz, seq, d = x.shape
    kw = conv_w.shape[-2]
    npair, pblk = w_a2.shape[-3], w_a2.shape[-1]
    ts, sub = RG_TS, RG_SUB
    assert seq % ts == 0 and ts % sub == 0 and kw - 1 <= sub // SUBLANES and npair * pblk == d
    perm = _time_perm(sub)
    return pl.pallas_call(
        functools.partial(_rg_kernel, ts=ts, sub=sub, d=d, conv_w=kw, npair=npair, pblk=pblk),
        grid=(bsz, seq // ts),
        in_specs=[
            pl.BlockSpec((None, ts, d), lambda b, s: (b, s, 0)),
            _const_spec((1, d)),
            _const_spec((sub, sub)),
            _const_spec((sub, sub)),
            _layer_spec((d, 2 * d), lead),
            _const_spec((kw, d)),
            _const_spec((1, d)),
            _layer_spec((npair, pblk, pblk), lead),
            _const_spec((1, d)),
            _layer_spec((npair, pblk, pblk), lead),
            _const_spec((1, d)),
            _const_spec((1, d)),
            _layer_spec((d, d), lead),
        ],
        out_specs=pl.BlockSpec((None, ts, d), lambda b, s: (b, s, 0)),
        out_shape=jax.ShapeDtypeStruct((bsz, seq, d), F32),
        scratch_shapes=[
            pltpu.VMEM(((kw - 1) * SUBLANES, d), F32),
            pltpu.VMEM((1, d), F32),
        ],
        compiler_params=_params("parallel", "arbitrary"),
        name="rglru",
    )(x, g, perm, perm.T, w_in, conv_w, conv_b, w_a2, b_a, w_x2, b_x, lam, w_out)


CV_TS = 512
CV_ROWS = 128
CV_HIST = 32


def _cv_kernel(x_ref, g_ref, win_ref, bin_ref, dw_ref, dwb_ref, lng_ref, lnb_ref, wout_ref, bout_ref,
               o_ref, ext_scr, y_scr, *, ts, d, kw):
    s = pl.program_id(1)

    @pl.when(s == 0)
    def _():
        ext_scr[0:CV_HIST, :] = jnp.zeros((CV_HIST, d), F32)

    x = x_ref[...]
    h = _rms(x, g_ref[...]).astype(BF16)
    val = _dot(h, win_ref[:, 0:d]) + bin_ref[:, 0:d]
    gate = _dot(h, win_ref[:, d:2 * d]) + bin_ref[:, d:2 * d]
    hh = val * _sigmoid(gate)
    ext_scr[CV_HIST:CV_HIST + ts, :] = hh

    def col_block(c, carry):
        c0 = pl.multiple_of(c * LANES, LANES)
        w = dw_ref[:, pl.ds(c0, LANES)]
        bias = dwb_ref[:, pl.ds(c0, LANES)]
        for r0 in range(0, ts, CV_ROWS):
            slab = ext_scr[r0:r0 + CV_ROWS + CV_HIST, pl.ds(c0, LANES)]
            acc = jnp.zeros((CV_ROWS, LANES), F32) + bias
            for r in range(SUBLANES):
                rolled = slab if r == 0 else pltpu.roll(slab, r, axis=0)
                for a in range(CV_HIST // SUBLANES):
                    dly = SUBLANES * a + r
                    if dly < kw:
                        lo = CV_HIST - SUBLANES * a
                        acc = acc + rolled[lo:lo + CV_ROWS, :] * w[kw - 1 - dly:kw - dly, :]
            y_scr[r0:r0 + CV_ROWS, pl.ds(c0, LANES)] = acc
        return carry

    lax.fori_loop(0, d // LANES, col_block, 0)
    ext_scr[0:CV_HIST, :] = hh[ts - CV_HIST:ts, :]

    y = y_scr[...]
    mu = jnp.mean(y, axis=-1, keepdims=True)
    yc = y - mu
    var = jnp.mean(yc * yc, axis=-1, keepdims=True)
    z = yc * lax.rsqrt(var + EPS) * lng_ref[...] + lnb_ref[...]
    z = (z * _sigmoid(z)).astype(BF16)
    o_ref[...] = x + _dot(z, wout_ref[...]) + bout_ref[...]


def _conv_module(x, g, w_in, b_in, dw_w, dw_b, ln_g, ln_b, w_out, b_out, lead):
    bsz, seq, d = x.shape
    kw = dw_w.shape[-2]
    ts = CV_TS
    assert seq % ts == 0 and kw - 1 <= CV_HIST and ts % CV_ROWS == 0
    return pl.pallas_call(
        functools.partial(_cv_kernel, ts=ts, d=d, kw=kw),
        grid=(bsz, seq // ts),
        in_specs=[
            pl.BlockSpec((None, ts, d), lambda b, s: (b, s, 0)),
            _const_spec((1, d)),
            _layer_spec((d, 2 * d), lead),
            _const_spec((1, 2 * d)),
            _const_spec((kw, d)),
            _const_spec((1, d)),
            _const_spec((1, d)),
            _const_spec((1, d)),
            _layer_spec((d, d), lead),
            _const_spec((1, d)),
        ],
        out_specs=pl.BlockSpec((None, ts, d), lambda b, s: (b, s, 0)),
        out_shape=jax.ShapeDtypeStruct((bsz, seq, d), F32),
        scratch_shapes=[
            pltpu.VMEM((ts + CV_HIST, d), F32),
            pltpu.VMEM((ts, d), F32),
        ],
        compiler_params=_params("parallel", "arbitrary"),
        name="convmod",
    )(x, g, w_in, b_in, dw_w, dw_b, ln_g, ln_b, w_out, b_out)


FX_TS = 512
FX_TQ = 512


def _head_rms(t, g2, lane_lo, hd):
    sq = t * t
    s_lo = jnp.sum(jnp.where(lane_lo, sq, 0.0), axis=-1, keepdims=True)
    s_hi = jnp.sum(jnp.where(lane_lo, 0.0, sq), axis=-1, keepdims=True)
    ms = jnp.where(lane_lo, s_lo, s_hi) * (1.0 / hd)
    return t * lax.rsqrt(ms + EPS) * g2


def _fxproj_kernel(x_ref, g_ref, wqkv_ref, wf_ref, bf_ref, qg_ref, kg_ref,
                   q_ref, kt_ref, v_ref, cum_ref, car_scr, *, ts, d, hd, nh):
    s = pl.program_id(1)

    @pl.when(s == 0)
    def _():
        car_scr[...] = jnp.zeros_like(car_scr)

    x = x_ref[...]
    h = _rms(x, g_ref[...]).astype(BF16)
    lane_lo = lax.broadcasted_iota(jnp.int32, (ts, LANES), 1) < hd
    scale = hd ** -0.5 * LOG2E
    q = _dot(h, wqkv_ref[:, 0:d])
    k = _dot(h, wqkv_ref[:, d:2 * d])
    for c in range(d // LANES):
        cols = slice(c * LANES, (c + 1) * LANES)
        q_ref[:, cols] = (_head_rms(q[:, cols], qg_ref[...], lane_lo, hd) * scale).astype(BF16)
        kt_ref[cols, :] = jnp.transpose(_head_rms(k[:, cols], kg_ref[...], lane_lo, hd)).astype(BF16)
    v_ref[...] = _dot(h, wqkv_ref[:, 2 * d:3 * d]).astype(BF16)

    f = _dot(h, wf_ref[...]) + bf_ref[...]
    cum = jnp.minimum(f, 0.0) - jnp.log1p(jnp.exp(-jnp.abs(f)))
    row = lax.broadcasted_iota(jnp.int32, (ts, LANES), 0)
    sh = 1
    while sh < ts:
        cum = cum + jnp.where(row >= sh, pltpu.roll(cum, sh, axis=0), 0.0)
        sh *= 2
    cum = cum + car_scr[...]
    car_scr[...] = cum[ts - 1:ts, :]
    cum_ref[...] = jnp.transpose(cum * LOG2E)[0:nh, :]


def _fox_proj(x, g, w_qkv, w_f, b_f, q_g2, k_g2, lead, nh):
    bsz, seq, d = x.shape
    hd = d // nh
    ts = FX_TS
    assert seq % ts == 0 and 2 * hd == LANES and nh % SUBLANES == 0
    tile = pl.BlockSpec((None, ts, d), lambda b, s: (b, s, 0))
    return pl.pallas_call(
        functools.partial(_fxproj_kernel, ts=ts, d=d, hd=hd, nh=nh),
        grid=(bsz, seq // ts),
        in_specs=[
            tile,
            _const_spec((1, d)),
            _layer_spec((d, 3 * d), lead),
            _layer_spec((d, LANES), lead),
            _const_spec((1, LANES)),
            _const_spec((1, LANES)),
            _const_spec((1, LANES)),
        ],
        out_specs=[tile, pl.BlockSpec((None, d, ts), lambda b, s: (b, 0, s)), tile,
                   pl.BlockSpec((None, nh, ts), lambda b, s: (b, 0, s))],
        out_shape=[
            jax.ShapeDtypeStruct((bsz, seq, d), BF16),
            jax.ShapeDtypeStruct((bsz, d, seq), BF16),
            jax.ShapeDtypeStruct((bsz, seq, d), BF16),
            jax.ShapeDtypeStruct((bsz, nh, seq), F32),
        ],
        scratch_shapes=[pltpu.VMEM((1, LANES), F32)],
        compiler_params=_params("parallel", "arbitrary"),
        name="foxproj",
    )(x, g, w_qkv, w_f, b_f, q_g2, k_g2)


def _fxattn_kernel(q_ref, kt_ref, v_ref, cum_ref, o_ref, m0_scr, m1_scr, acc0_scr, acc1_scr, *, tq, hd):
    i = pl.program_id(2)
    q = q_ref[...]
    lane_lo = lax.broadcasted_iota(jnp.int32, (tq, LANES), 1) < hd
    zero = jnp.zeros_like(q)
    qs = (jnp.where(lane_lo, q, zero), jnp.where(lane_lo, zero, q))
    m_scrs = (m0_scr, m1_scr)
    acc_scrs = (acc0_scr, acc1_scr)

    def scores(hh, j0, tk):
        return _dot(qs[hh], kt_ref[:, pl.ds(j0, tk)]) - cum_ref[hh:hh + 1, pl.ds(j0, tk)]

    def values(j0, tk):
        vb = v_ref[pl.ds(j0, tk), :]
        lo = lax.broadcasted_iota(jnp.int32, (tk, LANES), 1) < hd
        one = jnp.ones_like(vb)
        return (jnp.where(lo, vb, one), jnp.where(lo, one, vb))

    def row_max(sc):
        return jnp.broadcast_to(jnp.max(sc, axis=-1, keepdims=True), (tq, LANES))

    def lanes_to(m, tk):
        return jnp.concatenate([m] * (tk // LANES), axis=1)

    d0 = pl.multiple_of(i * tq, tq)
    half = tq // 2
    parts = [(r0, tk, hh) for r0, tk in ((0, half), (half, tq)) for hh in range(2)]
    vss = {tk: values(d0, tk) for tk in (half, tq)}
    scs = []
    for r0, tk, hh in parts:
        causal = (lax.broadcasted_iota(jnp.int32, (half, tk), 0) + r0
                  >= lax.broadcasted_iota(jnp.int32, (half, tk), 1))
        sc = _dot(qs[hh][r0:r0 + half, :], kt_ref[:, pl.ds(d0, tk)]) - cum_ref[hh:hh + 1, pl.ds(d0, tk)]
        scs.append(jnp.where(causal, sc, NEG_BIG))
    ms = [jnp.broadcast_to(jnp.max(sc, axis=-1, keepdims=True), (half, LANES)) for sc in scs]
    ps = [jnp.exp2(sc - lanes_to(m, tk)).astype(BF16) for sc, m, (r0, tk, hh) in zip(scs, ms, parts)]
    for p, m, (r0, tk, hh) in zip(ps, ms, parts):
        m_scrs[hh][r0:r0 + half, :] = m
        acc_scrs[hh][r0:r0 + half, :] = _dot(p, vss[tk][hh])

    def kv_block(j0, tk):
        vs = values(j0, tk)
        scs = [scores(hh, j0, tk) for hh in range(2)]
        m_olds = [m_scrs[hh][...] for hh in range(2)]
        m_news = [jnp.maximum(m_olds[hh], row_max(scs[hh])) for hh in range(2)]
        ps = [jnp.exp2(scs[hh] - lanes_to(m_news[hh], tk)).astype(BF16) for hh in range(2)]
        for hh in range(2):
            m_scrs[hh][...] = m_news[hh]
            acc_scrs[hh][...] = jnp.exp2(m_olds[hh] - m_news[hh]) * acc_scrs[hh][...] + _dot(ps[hh], vs[hh])

    def wide_block(j, carry):
        kv_block(pl.multiple_of(j * 2 * tq, 2 * tq), 2 * tq)
        return carry

    lax.fori_loop(0, i // 2, wide_block, 0)

    @pl.when(i % 2 == 1)
    def _():
        kv_block(pl.multiple_of((i - 1) * tq, tq), tq)

    acc_lo, acc_hi = acc0_scr[...], acc1_scr[...]
    o_lo = acc_lo / acc_lo[:, LANES - 1:LANES]
    o_hi = acc_hi / acc_hi[:, 0:1]
    o_ref[...] = jnp.where(lane_lo, o_lo, o_hi).astype(BF16)


def _fox_attn(q, kt, v, cum4, nh):
    bsz, seq, d = q.shape
    hd = d // nh
    tq = FX_TQ
    assert seq % tq == 0 and 2 * hd == LANES
    return pl.pallas_call(
        functools.partial(_fxattn_kernel, tq=tq, hd=hd),
        grid=(bsz, nh // 2, seq // tq),
        in_specs=[
            pl.BlockSpec((None, tq, LANES), lambda b, p, i: (b, i, p)),
            pl.BlockSpec((None, LANES, seq), lambda b, p, i: (b, p, 0)),
            pl.BlockSpec((None, seq, LANES), lambda b, p, i: (b, 0, p)),
            pl.BlockSpec((None, None, 2, seq), lambda b, p, i: (b, p, 0, 0)),
        ],
        out_specs=pl.BlockSpec((None, tq, LANES), lambda b, p, i: (b, i, p)),
        out_shape=jax.ShapeDtypeStruct((bsz, seq, d), BF16),
        scratch_shapes=[pltpu.VMEM((tq, LANES), F32)] * 4,
        compiler_params=_params("parallel", "parallel", "arbitrary"),
        name="foxattn",
    )(q, kt, v, cum4)


def kernel(x, ffn_norm, ffn_w_in, ffn_w_out, mix_norm, rg_w_in, rg_conv_w, rg_conv_b, rg_w_a, rg_b_a, rg_w_x,
           rg_b_x, rg_lambda, rg_w_out, fox_w_in, fox_b_f, fox_q_norm, fox_k_norm, fox_w_out, cv_w_in, cv_b_in,
           cv_dw_w, cv_dw_b, cv_ln_g, cv_ln_b, cv_w_out, cv_b_out):
    bsz, seq, d = x.shape
    depth = ffn_norm.shape[0]
    nh = fox_b_f.shape[-1]
    hd = fox_q_norm.shape[-1]
    assert nh * hd == d and nh <= LANES

    ffn_w_in_b = ffn_w_in.astype(BF16)
    ffn_w_out_b = ffn_w_out.astype(BF16)
    rg_w_in_b, rg_w_out_b = rg_w_in.astype(BF16), rg_w_out.astype(BF16)
    rg_w_a_b, rg_w_x_b = _pair_blocks(rg_w_a.astype(BF16)), _pair_blocks(rg_w_x.astype(BF16))
    fox_w_qkv_b = fox_w_in[:, :, :3 * d].astype(BF16)
    fox_w_f_b = jnp.pad(fox_w_in[:, :, 3 * d:], ((0, 0), (0, 0), (0, LANES - nh))).astype(BF16)
    fox_w_out_b = fox_w_out.astype(BF16)
    cv_w_in_b, cv_w_out_b = cv_w_in.astype(BF16), cv_w_out.astype(BF16)

    def row(p):
        return p[None, :]

    def pad_lanes(p):
        return jnp.pad(p, (0, LANES - p.shape[0]))[None, :]

    def twice(p):
        return jnp.concatenate([p, p])[None, :]

    for i in range(depth):
        kind, j = i % N_MIXERS, i // N_MIXERS
        x = _ffn(x.reshape(bsz * seq, d), row(ffn_norm[i, 0]), ffn_w_in_b, ffn_w_out_b, (i, 0))
        x = x.reshape(bsz, seq, d)
        g = row(mix_norm[i])
        proj = None
        if kind == 0:
            x = _rglru(x, g, rg_w_in_b, rg_conv_w[j], row(rg_conv_b[j]), rg_w_a_b, row(rg_b_a[j]), rg_w_x_b,
                       row(rg_b_x[j]), row(rg_lambda[j]), rg_w_out_b, (j,))
        elif kind == 1:
            q, kt, v, cum = _fox_proj(x, g, fox_w_qkv_b, fox_w_f_b, pad_lanes(fox_b_f[j]), twice(fox_q_norm[j]),
                                     twice(fox_k_norm[j]), (j,), nh)
            o = _fox_attn(q, kt, v, cum.reshape(bsz, nh // 2, 2, seq), nh)
            proj = (o.reshape(bsz * seq, d), fox_w_out_b, (j,))
        else:
            x = _conv_module(x, g, cv_w_in_b, row(cv_b_in[j]), cv_dw_w[j], row(cv_dw_b[j]), row(cv_ln_g[j]),
                             row(cv_ln_b[j]), cv_w_out_b, row(cv_b_out[j]), (j,))
        x = _ffn(x.reshape(bsz * seq, d), row(ffn_norm[i, 1]), ffn_w_in_b, ffn_w_out_b, (i, 1), proj)
        x = x.reshape(bsz, seq, d)
    return x
```

```python
import functools

import jax
import jax.numpy as jnp
from jax import lax
from jax.experimental import pallas as pl
from jax.experimental.pallas import tpu as pltpu

F32 = jnp.float32
BF16 = jnp.bfloat16

EPS = 1e-6
N_MIXERS = 3
RG_C = 8.0

LANES = 128
SUBLANES = 8
VMEM_LIMIT_BYTES = 56 * 1024 * 1024

NEG_BIG = -1e30
LOG2E = 1.4426950408889634


def _dot(a, b):
    return jnp.dot(a, b, preferred_element_type=F32)


def _sigmoid(x):
    return 1.0 / (1.0 + jnp.exp(-x))


def _rms(x, g):
    ms = jnp.mean(x * x, axis=-1, keepdims=True)
    return x * lax.rsqrt(ms + EPS) * g


def _softplus(z):
    return jnp.maximum(z, 0.0) + jnp.log1p(jnp.exp(-jnp.abs(z)))


def _params(*semantics):
    return pltpu.CompilerParams(dimension_semantics=semantics, vmem_limit_bytes=VMEM_LIMIT_BYTES)


def _const_spec(shape):
    n = len(shape)
    return pl.BlockSpec(shape, lambda *_: (0,) * n, pipeline_mode=pl.Buffered(1))


def _layer_spec(shape, lead):
    n = len(shape)
    block = (None,) * len(lead) + tuple(shape)
    return pl.BlockSpec(block, lambda *_: tuple(lead) + (0,) * n, pipeline_mode=pl.Buffered(1))


FFN_TM = 1024
FFN_TF = 256


def _swiglu_step(x, g_ref, win_ref, wout_ref, out_ref, d_ff, tf):
    h = _rms(x, g_ref[...]).astype(BF16)
    acc = None
    for c in range(d_ff // tf):
        gate = _dot(h, win_ref[:, c * tf:(c + 1) * tf])
        up = _dot(h, win_ref[:, d_ff + c * tf:d_ff + (c + 1) * tf])
        a = (gate * _sigmoid(gate) * up).astype(BF16)
        part = _dot(a, wout_ref[c * tf:(c + 1) * tf, :])
        acc = part if acc is None else acc + part
    out_ref[...] = x + 0.5 * acc


def _ffn_kernel(x_ref, g_ref, win_ref, wout_ref, out_ref, *, d_ff, tf):
    _swiglu_step(x_ref[...], g_ref, win_ref, wout_ref, out_ref, d_ff, tf)


def _proj_ffn_kernel(x_ref, o_ref, wo_ref, g_ref, win_ref, wout_ref, out_ref, *, d_ff, tf):
    x = x_ref[...] + _dot(o_ref[...], wo_ref[...])
    _swiglu_step(x, g_ref, win_ref, wout_ref, out_ref, d_ff, tf)


def _ffn(x2, g, w_in, w_out, lead, proj=None):
    t, d = x2.shape
    d_ff = w_out.shape[-2]
    assert t % FFN_TM == 0 and d_ff % FFN_TF == 0
    tile = pl.BlockSpec((FFN_TM, d), lambda i: (i, 0))
    ffn_specs = [_const_spec((1, d)), _layer_spec((d, 2 * d_ff), lead), _layer_spec((d_ff, d), lead)]
    if proj is None:
        body, in_specs, args = _ffn_kernel, [tile] + ffn_specs, (x2, g, w_in, w_out)
    else:
        o2, wo, wo_lead = proj
        body = _proj_ffn_kernel
        in_specs = [tile, tile, _layer_spec((d, d), wo_lead)] + ffn_specs
        args = (x2, o2, wo, g, w_in, w_out)
    return pl.pallas_call(
        functools.partial(body, d_ff=d_ff, tf=FFN_TF),
        grid=(t // FFN_TM,),
        in_specs=in_specs,
        out_specs=tile,
        out_shape=jax.ShapeDtypeStruct((t, d), F32),
        compiler_params=_params("parallel"),
        name="ffn" if proj is None else "projffn",
    )(*args)


RG_TS = 1024
RG_SUB = 256


def _rg_kernel(x_ref, g_ref, perm_ref, unperm_ref, win_ref, cw_ref, cb_ref, wa_ref, ba_ref, wx_ref, bx_ref, lam_ref,
               wout_ref, o_ref, tail_scr, hcar_scr, *, ts, sub, d, conv_w, npair, pblk):
    ngrp = sub // SUBLANES
    nhist = conv_w - 1
    nsub = ts // sub

    @pl.when(pl.program_id(1) == 0)
    def _():
        tail_scr[...] = jnp.zeros_like(tail_scr)
        hcar_scr[...] = jnp.zeros_like(hcar_scr)

    def grp(t, g):
        return t[g * SUBLANES:(g + 1) * SUBLANES, :]

    row = lax.broadcasted_iota(jnp.int32, (SUBLANES, d), 0)
    zscale = RG_C * _softplus(-lam_ref[...])

    xs, gates, us = [], [], []
    for i in range(nsub):
        x = x_ref[i * sub:(i + 1) * sub, :]
        hp = _dot(perm_ref[...], _rms(x, g_ref[...]).astype(BF16)).astype(BF16)
        xs.append(x)
        gates.append(_dot(hp, win_ref[:, 0:d]))
        us.append(_dot(hp, win_ref[:, d:2 * d]))

    ys = []
    prev_last = tail_scr[...]
    carry = hcar_scr[...]
    for i in range(nsub):
        u = us[i]
        last = u[sub - nhist * SUBLANES:sub, :]
        pre = [jnp.where(row == 0, pltpu.roll(grp(prev_last, k), 1, axis=0), pltpu.roll(grp(last, k), 1, axis=0))
               for k in range(nhist)]
        prev_last = last
        ext = jnp.concatenate(pre + [u], axis=0)
        uc = u * cw_ref[conv_w - 1:conv_w, :] + cb_ref[...]
        for dly in range(1, conv_w):
            lo = (nhist - dly) * SUBLANES
            uc = uc + ext[lo:lo + sub, :] * cw_ref[conv_w - 1 - dly:conv_w - dly, :]

        ub = uc.astype(BF16)
        ra = jnp.concatenate([_dot(ub[:, n * pblk:(n + 1) * pblk], wa_ref[n]) for n in range(npair)], axis=-1)
        rx = jnp.concatenate([_dot(ub[:, n * pblk:(n + 1) * pblk], wx_ref[n]) for n in range(npair)], axis=-1)
        r = _sigmoid(ra + ba_ref[...])
        ig = _sigmoid(rx + bx_ref[...])
        z = r * zscale
        a = jnp.exp(-z)
        b = jnp.sqrt(jnp.tanh(z) * (a * a + 1.0)) * (ig * uc)

        hl, al = [grp(b, 0)], [grp(a, 0)]
        for g in range(1, ngrp):
            ag = grp(a, g)
            hl.append(ag * hl[-1] + grp(b, g))
            al.append(ag * al[-1])
        ea, eh = al[-1], hl[-1]
        for sh in (1, 2, 4):
            keep = row >= sh
            ea_s = jnp.where(keep, pltpu.roll(ea, sh, axis=0), 1.0)
            eh_s = jnp.where(keep, pltpu.roll(eh, sh, axis=0), 0.0)
            eh = ea * eh_s + eh
            ea = ea * ea_s
        ends = ea * carry + eh
        cin = jnp.where(row == 0, carry, pltpu.roll(ends, 1, axis=0))
        carry = ends[SUBLANES - 1:SUBLANES, :]
        hs = jnp.concatenate([hl[g] + al[g] * cin for g in range(ngrp)], axis=0)
        ys.append((hs * jax.nn.gelu(gates[i])).astype(BF16))
    tail_scr[...] = prev_last
    hcar_scr[...] = carry

    for i in range(nsub):
        yn = _dot(unperm_ref[...], ys[i]).astype(BF16)
        o_ref[i * sub:(i + 1) * sub, :] = xs[i] + _dot(yn, wout_ref[...])


def _time_perm(n):
    r = jnp.arange(n)
    t = (r % SUBLANES) * (n // SUBLANES) + r // SUBLANES
    return (t[:, None] == jnp.arange(n)[None, :]).astype(BF16)


def _pair_blocks(w):
    l, nb, c, _ = w.shape
    w = w.reshape(l, nb // 2, 2, c, c)
    z = jnp.zeros_like(w[:, :, 0])
    top = jnp.concatenate([w[:, :, 0], z], axis=-1)
    bot = jnp.concatenate([z, w[:, :, 1]], axis=-1)
    return jnp.concatenate([top, bot], axis=-2)


def _rglru(x, g, w_in, conv_w, conv_b, w_a2, b_a, w_x2, b_x, lam, w_out, lead):
    bsz, seq, d = x.shape
    kw = conv_w.shape[-2]
    npair, pblk = w_a2.shape[-3], w_a2.shape[-1]
    ts, sub = RG_TS, RG_SUB
    assert seq % ts == 0 and ts % sub == 0 and kw - 1 <= sub // SUBLANES and npair * pblk == d
    perm = _time_perm(sub)
    return pl.pallas_call(
        functools.partial(_rg_kernel, ts=ts, sub=sub, d=d, conv_w=kw, npair=npair, pblk=pblk),
        grid=(bsz, seq // ts),
        in_specs=[
            pl.BlockSpec((None, ts, d), lambda b, s: (b, s, 0)),
            _const_spec((1, d)),
            _const_spec((sub, sub)),
            _const_spec((sub, sub)),
            _layer_spec((d, 2 * d), lead),
            _const_spec((kw, d)),
            _const_spec((1, d)),
            _layer_spec((npair, pblk, pblk), lead),
            _const_spec((1, d)),
            _layer_spec((npair, pblk, pblk), lead),
            _const_spec((1, d)),
            _const_spec((1, d)),
            _layer_spec((d, d), lead),
        ],
        out_specs=pl.BlockSpec((None, ts, d), lambda b, s: (b, s, 0)),
        out_shape=jax.ShapeDtypeStruct((bsz, seq, d), F32),
        scratch_shapes=[
            pltpu.VMEM(((kw - 1) * SUBLANES, d), F32),
            pltpu.VMEM((1, d), F32),
        ],
        compiler_params=_params("parallel", "arbitrary"),
        name="rglru",
    )(x, g, perm, perm.T, w_in, conv_w, conv_b, w_a2, b_a, w_x2, b_x, lam, w_out)


CV_TS = 1024
CV_ROWS = 128
CV_HIST = 32


def _cv_kernel(x_ref, g_ref, win_ref, bin_ref, dw_ref, dwb_ref, lng_ref, lnb_ref, wout_ref, bout_ref,
               o_ref, ext_scr, y_scr, *, ts, d, kw):
    s = pl.program_id(1)

    @pl.when(s == 0)
    def _():
        ext_scr[0:CV_HIST, :] = jnp.zeros((CV_HIST, d), F32)

    x = x_ref[...]
    h = _rms(x, g_ref[...]).astype(BF16)
    val = _dot(h, win_ref[:, 0:d]) + bin_ref[:, 0:d]
    gate = _dot(h, win_ref[:, d:2 * d]) + bin_ref[:, d:2 * d]
    hh = val * _sigmoid(gate)
    ext_scr[CV_HIST:CV_HIST + ts, :] = hh

    def col_block(c, carry):
        c0 = pl.multiple_of(c * LANES, LANES)
        w = dw_ref[:, pl.ds(c0, LANES)]
        bias = dwb_ref[:, pl.ds(c0, LANES)]
        for r0 in range(0, ts, CV_ROWS):
            slab = ext_scr[r0:r0 + CV_ROWS + CV_HIST, pl.ds(c0, LANES)]
            acc = jnp.zeros((CV_ROWS, LANES), F32) + bias
            for r in range(SUBLANES):
                rolled = slab if r == 0 else pltpu.roll(slab, r, axis=0)
                for a in range(CV_HIST // SUBLANES):
                    dly = SUBLANES * a + r
                    if dly < kw:
                        lo = CV_HIST - SUBLANES * a
                        acc = acc + rolled[lo:lo + CV_ROWS, :] * w[kw - 1 - dly:kw - dly, :]
            y_scr[r0:r0 + CV_ROWS, pl.ds(c0, LANES)] = acc
        return carry

    lax.fori_loop(0, d // LANES, col_block, 0)
    ext_scr[0:CV_HIST, :] = hh[ts - CV_HIST:ts, :]

    y = y_scr[...]
    mu = jnp.mean(y, axis=-1, keepdims=True)
    yc = y - mu
    var = jnp.mean(yc * yc, axis=-1, keepdims=True)
    z = yc * lax.rsqrt(var + EPS) * lng_ref[...] + lnb_ref[...]
    z = (z * _sigmoid(z)).astype(BF16)
    o_ref[...] = x + _dot(z, wout_ref[...]) + bout_ref[...]


def _conv_module(x, g, w_in, b_in, dw_w, dw_b, ln_g, ln_b, w_out, b_out, lead):
    bsz, seq, d = x.shape
    kw = dw_w.shape[-2]
    ts = CV_TS
    assert seq % ts == 0 and kw - 1 <= CV_HIST and ts % CV_ROWS == 0
    return pl.pallas_call(
        functools.partial(_cv_kernel, ts=ts, d=d, kw=kw),
        grid=(bsz, seq // ts),
        in_specs=[
            pl.BlockSpec((None, ts, d), lambda b, s: (b, s, 0)),
            _const_spec((1, d)),
            _layer_spec((d, 2 * d), lead),
            _const_spec((1, 2 * d)),
            _const_spec((kw, d)),
            _const_spec((1, d)),
            _const_spec((1, d)),
            _const_spec((1, d)),
            _layer_spec((d, d), lead),
            _const_spec((1, d)),
        ],
        out_specs=pl.BlockSpec((None, ts, d), lambda b, s: (b, s, 0)),
        out_shape=jax.ShapeDtypeStruct((bsz, seq, d), F32),
        scratch_shapes=[
            pltpu.VMEM((ts + CV_HIST, d), F32),
            pltpu.VMEM((ts, d), F32),
        ],
        compiler_params=_params("parallel", "arbitrary"),
        name="convmod",
    )(x, g, w_in, b_in, dw_w, dw_b, ln_g, ln_b, w_out, b_out)


FX_TS = 1024
FX_TQ = 512


def _head_rms(t, g2, lane_lo, hd):
    sq = t * t
    s_lo = jnp.sum(jnp.where(lane_lo, sq, 0.0), axis=-1, keepdims=True)
    s_hi = jnp.sum(jnp.where(lane_lo, 0.0, sq), axis=-1, keepdims=True)
    ms = jnp.where(lane_lo, s_lo, s_hi) * (1.0 / hd)
    return t * lax.rsqrt(ms + EPS) * g2


def _fxproj_kernel(x_ref, g_ref, wqkv_ref, wf_ref, bf_ref, qg_ref, kg_ref,
                   q_ref, kt_ref, v_ref, cum_ref, car_scr, *, ts, d, hd, nh):
    s = pl.program_id(1)

    @pl.when(s == 0)
    def _():
        car_scr[...] = jnp.zeros_like(car_scr)

    x = x_ref[...]
    h = _rms(x, g_ref[...]).astype(BF16)
    lane_lo = lax.broadcasted_iota(jnp.int32, (ts, LANES), 1) < hd
    scale = hd ** -0.5 * LOG2E
    q = _dot(h, wqkv_ref[:, 0:d])
    k = _dot(h, wqkv_ref[:, d:2 * d])
    for c in range(d // LANES):
        cols = slice(c * LANES, (c + 1) * LANES)
        q_ref[:, cols] = (_head_rms(q[:, cols], qg_ref[...], lane_lo, hd) * scale).astype(BF16)
        kt_ref[cols, :] = jnp.transpose(_head_rms(k[:, cols], kg_ref[...], lane_lo, hd)).astype(BF16)
    v_ref[...] = _dot(h, wqkv_ref[:, 2 * d:3 * d]).astype(BF16)

    f = _dot(h, wf_ref[...]) + bf_ref[...]
    cum = jnp.minimum(f, 0.0) - jnp.log1p(jnp.exp(-jnp.abs(f)))
    row = lax.broadcasted_iota(jnp.int32, (ts, LANES), 0)
    sh = 1
    while sh < ts:
        cum = cum + jnp.where(row >= sh, pltpu.roll(cum, sh, axis=0), 0.0)
        sh *= 2
    cum = cum + car_scr[...]
    car_scr[...] = cum[ts - 1:ts, :]
    cum_ref[...] = jnp.transpose(cum * LOG2E)[0:nh, :]


def _fox_proj(x, g, w_qkv, w_f, b_f, q_g2, k_g2, lead, nh):
    bsz, seq, d = x.shape
    hd = d // nh
    ts = FX_TS
    assert seq % ts == 0 and 2 * hd == LANES and nh % SUBLANES == 0
    tile = pl.BlockSpec((None, ts, d), lambda b, s: (b, s, 0))
    return pl.pallas_call(
        functools.partial(_fxproj_kernel, ts=ts, d=d, hd=hd, nh=nh),
        grid=(bsz, seq // ts),
        in_specs=[
            tile,
            _const_spec((1, d)),
            _layer_spec((d, 3 * d), lead),
            _layer_spec((d, LANES), lead),
            _const_spec((1, LANES)),
            _const_spec((1, LANES)),
            _const_spec((1, LANES)),
        ],
        out_specs=[tile, pl.BlockSpec((None, d, ts), lambda b, s: (b, 0, s)), tile,
                   pl.BlockSpec((None, nh, ts), lambda b, s: (b, 0, s))],
        out_shape=[
            jax.ShapeDtypeStruct((bsz, seq, d), BF16),
            jax.ShapeDtypeStruct((bsz, d, seq), BF16),
            jax.ShapeDtypeStruct((bsz, seq, d), BF16),
            jax.ShapeDtypeStruct((bsz, nh, seq), F32),
        ],
        scratch_shapes=[pltpu.VMEM((1, LANES), F32)],
        compiler_params=_params("parallel", "arbitrary"),
        name="foxproj",
    )(x, g, w_qkv, w_f, b_f, q_g2, k_g2)


def _fxattn_kernel(q_ref, kt_ref, v_ref, cum_ref, o_ref, m0_scr, m1_scr, acc0_scr, acc1_scr, *, tq, hd):
    i = pl.program_id(2)
    q = q_ref[...]
    lane_lo = lax.broadcasted_iota(jnp.int32, (tq, LANES), 1) < hd
    zero = jnp.zeros_like(q)
    qs = (jnp.where(lane_lo, q, zero), jnp.where(lane_lo, zero, q))
    m_scrs = (m0_scr, m1_scr)
    acc_scrs = (acc0_scr, acc1_scr)

    def scores(hh, j0, tk):
        return _dot(qs[hh], kt_ref[:, pl.ds(j0, tk)]) - cum_ref[hh:hh + 1, pl.ds(j0, tk)]

    def values(j0, tk):
        vb = v_ref[pl.ds(j0, tk), :]
        lo = lax.broadcasted_iota(jnp.int32, (tk, LANES), 1) < hd
        one = jnp.ones_like(vb)
        return (jnp.where(lo, vb, one), jnp.where(lo, one, vb))

    def row_max(sc):
        return jnp.broadcast_to(jnp.max(sc, axis=-1, keepdims=True), (tq, LANES))

    def lanes_to(m, tk):
        return jnp.concatenate([m] * (tk // LANES), axis=1)

    d0 = pl.multiple_of(i * tq, tq)
    half = tq // 2
    parts = [(r0, tk, hh) for r0, tk in ((0, half), (half, tq)) for hh in range(2)]
    vss = {tk: values(d0, tk) for tk in (half, tq)}
    scs = []
    for r0, tk, hh in parts:
        causal = (lax.broadcasted_iota(jnp.int32, (half, tk), 0) + r0
                  >= lax.broadcasted_iota(jnp.int32, (half, tk), 1))
        sc = _dot(qs[hh][r0:r0 + half, :], kt_ref[:, pl.ds(d0, tk)]) - cum_ref[hh:hh + 1, pl.ds(d0, tk)]
        scs.append(jnp.where(causal, sc, NEG_BIG))
    ms = [jnp.broadcast_to(jnp.max(sc, axis=-1, keepdims=True), (half, LANES)) for sc in scs]
    ps = [jnp.exp2(sc - lanes_to(m, tk)).astype(BF16) for sc, m, (r0, tk, hh) in zip(scs, ms, parts)]
    for p, m, (r0, tk, hh) in zip(ps, ms, parts):
        m_scrs[hh][r0:r0 + half, :] = m
        acc_scrs[hh][r0:r0 + half, :] = _dot(p, vss[tk][hh])

    def kv_block(j0, tk):
        vs = values(j0, tk)
        scs = [scores(hh, j0, tk) for hh in range(2)]
        m_olds = [m_scrs[hh][...] for hh in range(2)]
        m_news = [jnp.maximum(m_olds[hh], row_max(scs[hh])) for hh in range(2)]
        ps = [jnp.exp2(scs[hh] - lanes_to(m_news[hh], tk)).astype(BF16) for hh in range(2)]
        for hh in range(2):
            m_scrs[hh][...] = m_news[hh]
            acc_scrs[hh][...] = jnp.exp2(m_olds[hh] - m_news[hh]) * acc_scrs[hh][...] + _dot(ps[hh], vs[hh])

    def wide_block(j, carry):
        kv_block(pl.multiple_of(j * 2 * tq, 2 * tq), 2 * tq)
        return carry

    lax.fori_loop(0, i // 2, wide_block, 0)

    @pl.when(i % 2 == 1)
    def _():
        kv_block(pl.multiple_of((i - 1) * tq, tq), tq)

    acc_lo, acc_hi = acc0_scr[...], acc1_scr[...]
    o_lo = acc_lo / acc_lo[:, LANES - 1:LANES]
    o_hi = acc_hi / acc_hi[:, 0:1]
    o_ref[...] = jnp.where(lane_lo, o_lo, o_hi).astype(BF16)


def _fox_attn(q, kt, v, cum4, nh):
    bsz, seq, d = q.shape
    hd = d // nh
    tq = FX_TQ
    assert seq % tq == 0 and 2 * hd == LANES
    return pl.pallas_call(
        functools.partial(_fxattn_kernel, tq=tq, hd=hd),
        grid=(bsz, nh // 2, seq // tq),
        in_specs=[
            pl.BlockSpec((None, tq, LANES), lambda b, p, i: (b, i, p)),
            pl.BlockSpec((None, LANES, seq), lambda b, p, i: (b, p, 0)),
            pl.BlockSpec((None, seq, LANES), lambda b, p, i: (b, 0, p)),
            pl.BlockSpec((None, None, 2, seq), lambda b, p, i: (b, p, 0, 0)),
        ],
        out_specs=pl.BlockSpec((None, tq, LANES), lambda b, p, i: (b, i, p)),
        out_shape=jax.ShapeDtypeStruct((bsz, seq, d), BF16),
        scratch_shapes=[pltpu.VMEM((tq, LANES), F32)] * 4,
        compiler_params=_params("parallel", "parallel", "arbitrary"),
        name="foxattn",
    )(q, kt, v, cum4)


def kernel(x, ffn_norm, ffn_w_in, ffn_w_out, mix_norm, rg_w_in, rg_conv_w, rg_conv_b, rg_w_a, rg_b_a, rg_w_x,
           rg_b_x, rg_lambda, rg_w_out, fox_w_in, fox_b_f, fox_q_norm, fox_k_norm, fox_w_out, cv_w_in, cv_b_in,
           cv_dw_w, cv_dw_b, cv_ln_g, cv_ln_b, cv_w_out, cv_b_out):
    bsz, seq, d = x.shape
    depth = ffn_norm.shape[0]
    nh = fox_b_f.shape[-1]
    hd = fox_q_norm.shape[-1]
    assert nh * hd == d and nh <= LANES

    ffn_w_in_b = ffn_w_in.astype(BF16)
    ffn_w_out_b = ffn_w_out.astype(BF16)
    rg_w_in_b, rg_w_out_b = rg_w_in.astype(BF16), rg_w_out.astype(BF16)
    rg_w_a_b, rg_w_x_b = _pair_blocks(rg_w_a.astype(BF16)), _pair_blocks(rg_w_x.astype(BF16))
    fox_w_qkv_b = fox_w_in[:, :, :3 * d].astype(BF16)
    fox_w_f_b = jnp.pad(fox_w_in[:, :, 3 * d:], ((0, 0), (0, 0), (0, LANES - nh))).astype(BF16)
    fox_w_out_b = fox_w_out.astype(BF16)
    cv_w_in_b, cv_w_out_b = cv_w_in.astype(BF16), cv_w_out.astype(BF16)

    def row(p):
        return p[None, :]

    def pad_lanes(p):
        return jnp.pad(p, (0, LANES - p.shape[0]))[None, :]

    def twice(p):
        return jnp.concatenate([p, p])[None, :]

    for i in range(depth):
        kind, j = i % N_MIXERS, i // N_MIXERS
        x = _ffn(x.reshape(bsz * seq, d), row(ffn_norm[i, 0]), ffn_w_in_b, ffn_w_out_b, (i, 0))
        x = x.reshape(bsz, seq, d)
        g = row(mix_norm[i])
        proj = None
        if kind == 0:
            x = _rglru(x, g, rg_w_in_b, rg_conv_w[j], row(rg_conv_b[j]), rg_w_a_b, row(rg_b_a[j]), rg_w_x_b,
                       row(rg_b_x[j]), row(rg_lambda[j]), rg_w_out_b, (j,))
        elif kind == 1:
            q, kt, v, cum = _fox_proj(x, g, fox_w_qkv_b, fox_w_f_b, pad_lanes(fox_b_f[j]), twice(fox_q_norm[j]),
                                     twice(fox_k_norm[j]), (j,), nh)
            o = _fox_attn(q, kt, v, cum.reshape(bsz, nh // 2, 2, seq), nh)
            proj = (o.reshape(bsz * seq, d), fox_w_out_b, (j,))
        else:
            x = _conv_module(x, g, cv_w_in_b, row(cv_b_in[j]), cv_dw_w[j], row(cv_dw_b[j]), row(cv_ln_g[j]),
                             row(cv_ln_b[j]), cv_w_out_b, row(cv_b_out[j]), (j,))
        x = _ffn(x.reshape(bsz * seq, d), row(ffn_norm[i, 1]), ffn_w_in_b, ffn_w_out_b, (i, 1), proj)
        x = x.reshape(bsz, seq, d)
    return x
```
